```python
import math
import jax, jax.numpy as jnp
from jax import lax
import numpy as np

D_MODEL = 4096
BATCH = 1
SEQ = 8192
DEPTH = 1
DEC_BATCH = 32
DEC_SEQ = 4
PAST_LEN = 8192
PAGE_SIZE = 128

D_ATT = D_MODEL // 2
HEAD_DIM = 128
N_HEADS = D_ATT // HEAD_DIM
D_SSM = D_MODEL - D_ATT
GROUP_CH = 16
N_GROUPS = D_SSM // GROUP_CH
STATE_N = 64
D_MIX = D_ATT + D_SSM
D_IN = 4 * D_ATT + 2 * D_SSM
PLE_DIM = 256
Q_BLOCK = 128
SB_BIAS_INIT = -5.0
EPS = 1e-6

kernel_name = "hymba_stickbreak_s5_decode_step"


def rmsnorm(x, g):
    xf = x.astype(jnp.float32)
    y = xf * lax.rsqrt(jnp.mean(xf * xf, axis=-1, keepdims=True) + EPS) * g.astype(jnp.float32)
    return y.astype(x.dtype)


def stick_breaking(q, k, v, bias, q_offset):
    B, Tq, H, Dh = q.shape
    Tk = k.shape[1]
    qb = min(Q_BLOCK, Tq)
    nb = Tq // qb
    scale = 1.0 / math.sqrt(Dh)
    qs = q.reshape(B, nb, qb, H, Dh).transpose(1, 0, 2, 3, 4)
    starts = jnp.arange(nb, dtype=jnp.int32) * qb
    key_pos = jnp.arange(Tk, dtype=jnp.int32)
    bias_f = bias.astype(jnp.float32)[None, :, None, None]

    def block(args):
        qblk, start = args
        z = jnp.einsum('bqhd,bkhd->bhqk', qblk, k).astype(jnp.float32) * scale + bias_f
        q_pos = q_offset + start + jnp.arange(qb, dtype=jnp.int32)
        causal = key_pos[None, :] < q_pos[:, None]
        log_1m = jnp.where(causal, jax.nn.log_sigmoid(-z), 0.0)
        tail = lax.cumsum(log_1m, axis=3, reverse=True) - log_1m
        w = jnp.where(causal, jnp.exp(jax.nn.log_sigmoid(z) + tail), 0.0)
        return jnp.einsum('bhqk,bkhd->bqhd', w.astype(v.dtype), v)

    out = lax.map(block, (qs, starts))
    return out.transpose(1, 0, 2, 3, 4).reshape(B, Tq, H, Dh)


def s5_scan(u, h0_re, h0_im, a_re, a_im, log_dt, b_re, b_im, c_re, c_im, d_skip):
    f32 = jnp.float32
    u = u.astype(f32)
    a_re = a_re.astype(f32); a_im = a_im.astype(f32)
    dt = jnp.exp(log_dt.astype(f32))[:, None]
    mag = jnp.exp(a_re * dt)
    ang = a_im * dt
    lb_re = mag * jnp.cos(ang); lb_im = mag * jnp.sin(ang)
    den = a_re * a_re + a_im * a_im
    num_re = lb_re - 1.0
    f_re = (num_re * a_re + lb_im * a_im) / den
    f_im = (lb_im * a_re - num_re * a_im) / den
    b_re = b_re.astype(f32); b_im = b_im.astype(f32)
    bb_re = f_re[..., None] * b_re - f_im[..., None] * b_im
    bb_im = f_re[..., None] * b_im + f_im[..., None] * b_re
    bu_re = jnp.einsum('btgp,gnp->btgn', u, bb_re)
    bu_im = jnp.einsum('btgp,gnp->btgn', u, bb_im)
    h0_re = h0_re.astype(f32); h0_im = h0_im.astype(f32)
    bu_re = bu_re.at[:, 0].add(lb_re * h0_re - lb_im * h0_im)
    bu_im = bu_im.at[:, 0].add(lb_re * h0_im + lb_im * h0_re)
    a_r = jnp.broadcast_to(lb_re, bu_re.shape)
    a_i = jnp.broadcast_to(lb_im, bu_im.shape)

    def combine(e1, e2):
        a1r, a1i, b1r, b1i = e1
        a2r, a2i, b2r, b2i = e2
        return (a2r * a1r - a2i * a1i,
                a2r * a1i + a2i * a1r,
                a2r * b1r - a2i * b1i + b2r,
                a2r * b1i + a2i * b1r + b2i)

    _, _, hr, hi = lax.associative_scan(combine, (a_r, a_i, bu_re, bu_im), axis=1)
    y = (jnp.einsum('btgn,gpn->btgp', hr, c_re.astype(f32))
         - jnp.einsum('btgn,gpn->btgp', hi, c_im.astype(f32))
         + d_skip.astype(f32) * u)
    return y, hr[:, -1], hi[:, -1]


def layer(h, p, k_past, v_past, h0_re, h0_im, q_offset,
          g_in, w_in, att_bias, a_re, a_im, log_dt, b_re, b_im, c_re, c_im, d_skip,
          w_glu, b_glu, g_att, g_ssm, w_out, g_ple, w_ple, w_ple_gate):
    B, T, _ = h.shape
    xn = rmsnorm(h, g_in)
    proj = xn @ w_in
    q, k, v, ga, u, gs = jnp.split(
        proj, [D_ATT, 2 * D_ATT, 3 * D_ATT, 4 * D_ATT, 4 * D_ATT + D_SSM], axis=-1)
    q = q.reshape(B, T, N_HEADS, HEAD_DIM)
    k = k.reshape(B, T, N_HEADS, HEAD_DIM)
    v = v.reshape(B, T, N_HEADS, HEAD_DIM)
    if k_past is None:
        k_all, v_all = k, v
    else:
        k_all = jnp.concatenate([k_past.astype(k.dtype), k], axis=1)
        v_all = jnp.concatenate([v_past.astype(v.dtype), v], axis=1)
    att = stick_breaking(q, k_all, v_all, att_bias, q_offset).reshape(B, T, D_ATT)
    att = rmsnorm(att, g_att) * jax.nn.silu(ga)

    y, hr, hi = s5_scan(u.reshape(B, T, N_GROUPS, GROUP_CH), h0_re, h0_im,
                        a_re, a_im, log_dt, b_re, b_im, c_re, c_im, d_skip)
    y = jax.nn.gelu(y.reshape(B, T, D_SSM)).astype(h.dtype)
    y = y * jax.nn.sigmoid(y @ w_glu + b_glu)
    ssm = rmsnorm(y, g_ssm) * jax.nn.silu(gs)

    h = h + jnp.concatenate([att, ssm], axis=-1) @ w_out
    h = h + (p.astype(h.dtype) @ w_ple) * jax.nn.sigmoid(rmsnorm(h, g_ple) @ w_ple_gate)
    return h, k, v, hr, hi


def setup_inputs(seed: int = 0) -> dict:
    key = jax.random.key(seed)
    ks = iter(jax.random.split(key, 40))
    f32 = jnp.float32
    n_pages = PAST_LEN // PAGE_SIZE
    n_pool = (DEC_BATCH * n_pages * 5) // 4
    nrm = lambda shape, s=1.0: jax.random.normal(next(ks), shape, f32) * s

    perm = jax.random.permutation(next(ks), n_pool)[:DEC_BATCH * n_pages]
    page_table = perm.reshape(DEC_BATCH, n_pages).astype(jnp.int32)

    n_idx = jnp.arange(STATE_N, dtype=f32)
    a_re = -0.5 + 0.01 * nrm((DEPTH, N_GROUPS, STATE_N))
    a_im = math.pi * n_idx[None, None, :] + 0.01 * nrm((DEPTH, N_GROUPS, STATE_N))
    log_dt = jax.random.uniform(next(ks), (DEPTH, N_GROUPS), f32,
                                math.log(1e-3), math.log(1e-1))
    return {
        "x_prompt": nrm((BATCH, SEQ, D_MODEL)),
        "x_sample": nrm((DEC_BATCH, DEC_SEQ, D_MODEL)),
        "cache_k": nrm((DEPTH, n_pool, PAGE_SIZE, N_HEADS, HEAD_DIM)),
        "cache_v": nrm((DEPTH, n_pool, PAGE_SIZE, N_HEADS, HEAD_DIM)),
        "state_ssm_re": nrm((DEPTH, DEC_BATCH, N_GROUPS, STATE_N), 0.3),
        "state_ssm_im": nrm((DEPTH, DEC_BATCH, N_GROUPS, STATE_N), 0.3),
        "page_table": page_table,
        "p_prompt": nrm((DEPTH, BATCH, SEQ, PLE_DIM)),
        "p_sample": nrm((DEPTH, DEC_BATCH, DEC_SEQ, PLE_DIM)),
        "g_in": 1.0 + 0.02 * nrm((DEPTH, D_MODEL)),
        "w_in": nrm((DEPTH, D_MODEL, D_IN), D_MODEL ** -0.5),
        "att_bias": SB_BIAS_INIT + 0.1 * nrm((DEPTH, N_HEADS)),
        "a_re": a_re,
        "a_im": a_im,
        "log_dt": log_dt,
        "b_re": nrm((DEPTH, N_GROUPS, STATE_N, GROUP_CH), (2 * GROUP_CH) ** -0.5),
        "b_im": nrm((DEPTH, N_GROUPS, STATE_N, GROUP_CH), (2 * GROUP_CH) ** -0.5),
        "c_re": nrm((DEPTH, N_GROUPS, GROUP_CH, STATE_N), (2 * STATE_N) ** -0.5),
        "c_im": nrm((DEPTH, N_GROUPS, GROUP_CH, STATE_N), (2 * STATE_N) ** -0.5),
        "d_skip": nrm((DEPTH, N_GROUPS, GROUP_CH)),
        "w_glu": nrm((DEPTH, D_SSM, D_SSM), D_SSM ** -0.5),
        "b_glu": nrm((DEPTH, D_SSM), 0.01),
        "g_att": 1.0 + 0.02 * nrm((DEPTH, D_ATT)),
        "g_ssm": 1.0 + 0.02 * nrm((DEPTH, D_SSM)),
        "w_out": nrm((DEPTH, D_MIX, D_MODEL), D_MIX ** -0.5),
        "g_ple": 1.0 + 0.02 * nrm((DEPTH, D_MODEL)),
        "w_ple": nrm((DEPTH, PLE_DIM, D_MODEL), PLE_DIM ** -0.5),
        "w_ple_gate": nrm((DEPTH, D_MODEL, D_MODEL), D_MODEL ** -0.5),
        "g_final": 1.0 + 0.02 * nrm((D_MODEL,)),
    }


def reference(x_prompt, x_sample, cache_k, cache_v, state_ssm_re, state_ssm_im, page_table,
              p_prompt, p_sample, g_in, w_in, att_bias, a_re, a_im, log_dt, b_re, b_im, c_re, c_im,
              d_skip, w_glu, b_glu, g_att, g_ssm, w_out, g_ple, w_ple, w_ple_gate, g_final):
    n_dec, n_pages = page_table.shape
    past_len = n_pages * cache_k.shape[2]
    hp, hs = x_prompt, x_sample
    kp_l, vp_l, ks_l, vs_l, rp_l, ip_l, rs_l, is_l = [], [], [], [], [], [], [], []
    zeros_state = jnp.zeros((x_prompt.shape[0], N_GROUPS, STATE_N), jnp.float32)
    for i in range(DEPTH):
        w = (g_in[i], w_in[i], att_bias[i], a_re[i], a_im[i], log_dt[i], b_re[i], b_im[i],
             c_re[i], c_im[i], d_skip[i], w_glu[i], b_glu[i], g_att[i], g_ssm[i], w_out[i],
             g_ple[i], w_ple[i], w_ple_gate[i])
        hp, kp, vp, rp, ip = layer(hp, p_prompt[i], None, None, zeros_state, zeros_state, 0, *w)
        k_past = cache_k[i][page_table].reshape(n_dec, past_len, N_HEADS, HEAD_DIM)
        v_past = cache_v[i][page_table].reshape(n_dec, past_len, N_HEADS, HEAD_DIM)
        hs, ks_, vs_, rs, is_ = layer(hs, p_sample[i], k_past, v_past,
                                      state_ssm_re[i], state_ssm_im[i], past_len, *w)
        kp_l.append(kp); vp_l.append(vp); ks_l.append(ks_); vs_l.append(vs_)
        rp_l.append(rp); ip_l.append(ip); rs_l.append(rs); is_l.append(is_)
    y_prompt = rmsnorm(hp, g_final)
    y_sample = rmsnorm(hs, g_final)
    new_k_prompt = jnp.stack(kp_l); new_v_prompt = jnp.stack(vp_l)
    new_k_sample = jnp.stack(ks_l); new_v_sample = jnp.stack(vs_l)
    new_ssm_re_prompt = jnp.stack(rp_l); new_ssm_im_prompt = jnp.stack(ip_l)
    new_ssm_re_sample = jnp.stack(rs_l); new_ssm_im_sample = jnp.stack(is_l)
    return (y_prompt, y_sample, new_k_prompt, new_v_prompt, new_k_sample, new_v_sample,
            new_ssm_re_prompt, new_ssm_im_prompt, new_ssm_re_sample, new_ssm_im_sample)
```

```python
import functools
import math

import jax
import jax.numpy as jnp
from jax import lax
from jax.experimental import pallas as pl
from jax.experimental.pallas import tpu as pltpu

F32 = jnp.float32
BF16 = jnp.bfloat16

HEAD_DIM = 128
GROUP_CH = 16
STATE_N = 64
EPS = 1e-6
LANES = 128
VMEM_LIMIT = 56 * 1024 * 1024
GROUPS_PER_QUAD = LANES // GROUP_CH
PAIRS_PER_QUAD = GROUPS_PER_QUAD // 2


def _params(*sem):
    return pltpu.CompilerParams(dimension_semantics=sem, vmem_limit_bytes=VMEM_LIMIT)


def _dot(a, b):
    return jnp.dot(a, b, preferred_element_type=F32)


def _split_bf16(x):
    hi = x.astype(BF16)
    lo = (x - hi.astype(F32)).astype(BF16)
    return hi, lo


def _dot_x3(a, b):
    ah, al = _split_bf16(a)
    bh, bl = _split_bf16(b)
    return _dot(ah, bh) + _dot(al, bh) + _dot(ah, bl)


def _softplus(z):
    return jnp.maximum(z, 0.0) + jnp.log(1.0 + jnp.exp(-jnp.abs(z)))


def _rms_kernel(x_ref, g_ref, o_ref):
    x = x_ref[...]
    ms = jnp.mean(x * x, axis=-1, keepdims=True)
    o_ref[...] = (x * lax.rsqrt(ms + EPS) * g_ref[...]).astype(o_ref.dtype)


def rmsnorm_rows(x, g, out_dtype, tm=256):
    m, d = x.shape
    tm = min(tm, m)
    return pl.pallas_call(
        _rms_kernel,
        grid=(m // tm,),
        in_specs=[pl.BlockSpec((tm, d), lambda i: (i, 0)),
                  pl.BlockSpec((1, d), lambda i: (0, 0))],
        out_specs=pl.BlockSpec((tm, d), lambda i: (i, 0)),
        out_shape=jax.ShapeDtypeStruct((m, d), out_dtype),
        compiler_params=_params("parallel"),
        name="rmsnorm",
    )(x, g.reshape(1, d))


def _proj_kernel(x_ref, w_ref, *o_refs, scale):
    acc = _dot(x_ref[...], w_ref[...])
    if scale != 1.0:
        acc = acc * scale
    for o_ref in o_refs:
        o_ref[...] = acc.astype(o_ref.dtype)


def project(xn, w, col0, n, out_dtypes, scale=1.0, tm=1024, tn=1024):
    m, k = xn.shape
    tm = min(tm, m)
    tn = min(tn, n)
    assert col0 % tn == 0 and n % tn == 0
    off = col0 // tn
    return pl.pallas_call(
        functools.partial(_proj_kernel, scale=scale),
        grid=(m // tm, n // tn),
        in_specs=[pl.BlockSpec((tm, k), lambda i, j: (i, 0)),
                  pl.BlockSpec((k, tn), lambda i, j: (0, off + j))],
        out_specs=[pl.BlockSpec((tm, tn), lambda i, j: (i, j)) for _ in out_dtypes],
        out_shape=[jax.ShapeDtypeStruct((m, n), dt) for dt in out_dtypes],
        compiler_params=_params("parallel", "arbitrary"),
        name="in_proj",
    )(xn, w)


def _tail_matrix(tk):
    j = jnp.arange(tk)[:, None]
    s = jnp.arange(tk)[None, :]
    u = (j > s).astype(BF16)
    return jnp.concatenate([u, jnp.ones((tk, LANES), BF16)], axis=1)


def _sb_tile(q, k, v, bias, uo, r, mask, tk):
    s = lax.dot_general(q, k, (((1,), (1,)), ((), ())), preferred_element_type=F32)
    z = s + bias
    sp = _softplus(z)
    if mask is not None:
        sp = jnp.where(mask, sp, 0.0)
    hi, lo = _split_bf16(sp)
    t2 = _dot(hi, uo) + _dot(lo, uo)
    rb = jnp.concatenate([r] * (tk // LANES), axis=1) if tk > LANES else r
    w = jnp.exp(z - sp - t2[:, :tk] - rb)
    if mask is not None:
        w = jnp.where(mask, w, 0.0)
    return _dot(w.astype(BF16), v), t2[:, tk:]


def _attn_kernel(bias_ref, q_ref, k_ref, v_ref, uo_ref, o_ref, acc_ref, r_ref, *, tq, tk):
    h = pl.program_id(0)
    qi = pl.program_id(1)
    bias = bias_ref[h]
    q = q_ref[...]
    uo = uo_ref[...]
    acc_ref[...] = jnp.zeros_like(acc_ref)
    r_ref[...] = jnp.zeros_like(r_ref)
    nd = tq // tk

    def tile(kt, mask):
        start = pl.multiple_of(kt * tk, tk)
        k = k_ref[pl.ds(start, tk), :]
        v = v_ref[pl.ds(start, tk), :]
        r = r_ref[...]
        pv, rs = _sb_tile(q, k, v, bias, uo, r, mask, tk)
        acc_ref[...] += pv
        r_ref[...] = r + rs

    row = lax.broadcasted_iota(jnp.int32, (tq, tk), 0)
    col = lax.broadcasted_iota(jnp.int32, (tq, tk), 1)
    for d in range(nd - 1, -1, -1):
        tile(qi * nd + d, (col + d * tk) < row)

    def body(i, carry):
        tile(qi * nd - 1 - i, None)
        return carry

    lax.fori_loop(0, qi * nd, body, 0)
    o_ref[...] = acc_ref[...]


def attention_prompt(q, k, v, bias, tq=512, tk=256):
    t, d = q.shape
    nh = d // HEAD_DIM
    tq = min(tq, t)
    tk = min(tk, tq)
    return pl.pallas_call(
        functools.partial(_attn_kernel, tq=tq, tk=tk),
        grid=(nh, t // tq),
        in_specs=[pl.BlockSpec(memory_space=pltpu.SMEM),
                  pl.BlockSpec((tq, HEAD_DIM), lambda h, i: (i, h)),
                  pl.BlockSpec((t, HEAD_DIM), lambda h, i: (0, h)),
                  pl.BlockSpec((t, HEAD_DIM), lambda h, i: (0, h)),
                  pl.BlockSpec((tk, tk + LANES), lambda h, i: (0, 0))],
        out_specs=pl.BlockSpec((tq, HEAD_DIM), lambda h, i: (i, h)),
        out_shape=jax.ShapeDtypeStruct((t, d), F32),
        scratch_shapes=[pltpu.VMEM((tq, HEAD_DIM), F32), pltpu.VMEM((tq, LANES), F32)],
        compiler_params=_params("parallel", "arbitrary"),
        name="sb_attention_prompt",
    )(bias, q, k, v, _tail_matrix(tk))


def _dec_attn_kernel(pt_ref, qbd_ref, bias_ref, knew_ref, vnew_ref, kpg_ref, vpg_ref, uo_ref,
                     o_ref, acc_ref, r_ref, kpad_ref, vpad_ref, *, n_pages, nt, page):
    del pt_ref
    j = pl.program_id(1)
    qbd = qbd_ref[...]
    bias = bias_ref[...]
    uo = uo_ref[...]
    rows = qbd.shape[0]

    def process(k, v, mask):
        r = r_ref[...]
        pv, rs = _sb_tile(qbd, k.astype(BF16), v.astype(BF16), bias, uo, r, mask, page)
        acc_ref[...] += pv
        r_ref[...] = r + rs

    @pl.when(j == 0)
    def _():
        acc_ref[...] = jnp.zeros_like(acc_ref)
        r_ref[...] = jnp.zeros_like(r_ref)
        kpad_ref[...] = jnp.zeros_like(kpad_ref)
        vpad_ref[...] = jnp.zeros_like(vpad_ref)
        kpad_ref[0:nt, :] = knew_ref[...]
        vpad_ref[0:nt, :] = vnew_ref[...]
        tok = lax.broadcasted_iota(jnp.int32, (rows, page), 0) % nt
        key = lax.broadcasted_iota(jnp.int32, (rows, page), 1)
        process(kpad_ref[...], vpad_ref[...], key < tok)

    @pl.when(j > 0)
    def _():
        process(kpg_ref[...], vpg_ref[...], None)

    @pl.when(j == n_pages)
    def _():
        acc = acc_ref[...]
        for h in range(rows // nt):
            o_ref[:, h * HEAD_DIM:(h + 1) * HEAD_DIM] = (
                acc[h * nt:(h + 1) * nt, h * HEAD_DIM:(h + 1) * HEAD_DIM])


def attention_sample(q, k_new, v_new, cache_k, cache_v, page_table, bias):
    b, nt, d = q.shape
    nh = d // HEAD_DIM
    page = cache_k.shape[1]
    n_pages = page_table.shape[1]
    rows = nh * nt
    qh = q.reshape(b, nt, nh, HEAD_DIM).transpose(0, 2, 1, 3)
    eye = jnp.eye(nh, dtype=q.dtype)
    qbd = (qh[:, :, :, None, :] * eye[None, :, None, :, None]).reshape(b, rows, d)
    bias_rows = jnp.broadcast_to(jnp.repeat(bias, nt)[:, None], (rows, LANES)).astype(F32)

    def page_map(i, j, pt):
        return (pt[i, n_pages - jnp.maximum(j, 1)], 0, 0)

    grid_spec = pltpu.PrefetchScalarGridSpec(
        num_scalar_prefetch=1,
        grid=(b, n_pages + 1),
        in_specs=[pl.BlockSpec((None, rows, d), lambda i, j, pt: (i, 0, 0)),
                  pl.BlockSpec((rows, LANES), lambda i, j, pt: (0, 0)),
                  pl.BlockSpec((None, nt, d), lambda i, j, pt: (i, 0, 0)),
                  pl.BlockSpec((None, nt, d), lambda i, j, pt: (i, 0, 0)),
                  pl.BlockSpec((None, page, d), page_map),
                  pl.BlockSpec((None, page, d), page_map),
                  pl.BlockSpec((page, page + LANES), lambda i, j, pt: (0, 0))],
        out_specs=pl.BlockSpec((None, nt, d), lambda i, j, pt: (i, 0, 0)),
        scratch_shapes=[pltpu.VMEM((rows, d), F32), pltpu.VMEM((rows, LANES), F32),
                        pltpu.VMEM((page, d), F32), pltpu.VMEM((page, d), F32)],
    )
    return pl.pallas_call(
        functools.partial(_dec_attn_kernel, n_pages=n_pages, nt=nt, page=page),
        grid_spec=grid_spec,
        out_shape=jax.ShapeDtypeStruct((b, nt, d), F32),
        compiler_params=_params("parallel", "arbitrary"),
        name="sb_attention_sample",
    )(page_table, qbd, bias_rows, k_new, v_new, cache_k, cache_v, _tail_matrix(page))


def _s5_discretize_kernel(are_ref, aim_ref, ldt_ref, bre_ref, bim_ref,
                          lbre_ref, lbim_ref, bbre_ref, bbim_ref):
    a_re = are_ref[...]
    a_im = aim_ref[...]
    dt = jnp.exp(ldt_ref[...])
    mag = jnp.exp(a_re * dt)
    ang = a_im * dt
    lb_re = mag * jnp.cos(ang)
    lb_im = mag * jnp.sin(ang)
    den = a_re * a_re + a_im * a_im
    num_re = lb_re - 1.0
    f_re = (num_re * a_re + lb_im * a_im) / den
    f_im = (lb_im * a_re - num_re * a_im) / den
    lbre_ref[...] = lb_re
    lbim_ref[...] = lb_im
    b_re = bre_ref[...]
    b_im = bim_ref[...]
    bbre_ref[...] = f_re * b_re - f_im * b_im
    bbim_ref[...] = f_re * b_im + f_im * b_re


def s5_discretize(a_re, a_im, log_dt, b_re, b_im):
    g, n = a_re.shape
    p = b_re.shape[-1]
    rep = lambda x: jnp.repeat(x, p, axis=0)
    rows = lambda x: x.transpose(0, 2, 1).reshape(g * p, n)
    sd = jax.ShapeDtypeStruct((g * p, n), F32)
    lb_re, lb_im, bb_re, bb_im = pl.pallas_call(
        _s5_discretize_kernel,
        out_shape=[sd, sd, sd, sd],
        name="s5_discretize",
    )(rep(a_re), rep(a_im), rep(jnp.broadcast_to(log_dt[:, None], (g, n))), rows(b_re), rows(b_im))
    return lb_re[::p], lb_im[::p], bb_re.reshape(g, p, n), bb_im.reshape(g, p, n)


def _s5_block_operands(bb_re, bb_im, c_re, c_im):
    g, p, n = bb_re.shape
    nq = g // GROUPS_PER_QUAD
    eye = jnp.eye(GROUPS_PER_QUAD, dtype=F32)

    def b_side(x):
        x = x.reshape(nq, GROUPS_PER_QUAD, p, n)
        return (x[:, :, :, None, :] * eye[None, :, None, :, None]).reshape(
            nq, GROUPS_PER_QUAD * p, GROUPS_PER_QUAD * n)

    def c_side(x):
        x = x.reshape(nq, GROUPS_PER_QUAD, p, n).transpose(0, 1, 3, 2)
        return (x[:, :, :, None, :] * eye[None, :, None, :, None]).reshape(
            nq, GROUPS_PER_QUAD * n, GROUPS_PER_QUAD * p)

    bq = jnp.concatenate([b_side(bb_re), b_side(bb_im)], axis=2)
    cq = jnp.concatenate([c_side(c_re), -c_side(c_im)], axis=1)
    return bq, cq


def _s5_input_phase(u_ref, bq_ref, bre_scr, bim_scr, rows, precise):
    nq = bq_ref.shape[0]
    half = PAIRS_PER_QUAD * LANES
    for qd in range(nq):
        uq = u_ref[:, qd * LANES:(qd + 1) * LANES]
        if precise:
            bu = _dot_x3(uq, bq_ref[qd])
        else:
            bu = _dot(uq.astype(BF16), bq_ref[qd])
        for jj in range(PAIRS_PER_QUAD):
            pr = PAIRS_PER_QUAD * qd + jj
            bre_scr[pr * rows:(pr + 1) * rows, :] = bu[:, jj * LANES:(jj + 1) * LANES]
            bim_scr[pr * rows:(pr + 1) * rows, :] = bu[:, half + jj * LANES:half + (jj + 1) * LANES]


def _s5_output_phase(u_ref, cq_ref, dskip_ref, y_ref, hre_scr, him_scr, rows, precise):
    nq = cq_ref.shape[0]
    for qd in range(nq):
        parts = []
        for scr in (hre_scr, him_scr):
            for jj in range(PAIRS_PER_QUAD):
                pr = PAIRS_PER_QUAD * qd + jj
                parts.append(scr[pr * rows:(pr + 1) * rows, :])
        hcat = jnp.concatenate(parts, axis=1)
        if precise:
            yq = _dot_x3(hcat, cq_ref[qd])
        else:
            yq = _dot(hcat.astype(BF16), cq_ref[qd])
        sl = slice(qd * LANES, (qd + 1) * LANES)
        y_ref[:, sl] = yq + dskip_ref[:, sl] * u_ref[:, sl]


def _s5_prompt_kernel(u_ref, bq_ref, cq_ref, lbre_ref, lbim_ref, dskip_ref, h0re_ref, h0im_ref,
                      y_ref, hre_out, him_out, bre_scr, bim_scr, stre_scr, stim_scr, *, tc):
    i = pl.program_id(0)
    n_pairs = lbre_ref.shape[0]

    @pl.when(i == 0)
    def _():
        stre_scr[...] = h0re_ref[...]
        stim_scr[...] = h0im_ref[...]

    _s5_input_phase(u_ref, bq_ref, bre_scr, bim_scr, tc, False)

    lr = lbre_ref[...]
    li = lbim_ref[...]

    def step(t, carry):
        hr, hi = carry
        idx = pl.ds(t, n_pairs, stride=tc)
        nr = lr * hr - li * hi + bre_scr[idx, :]
        ni = lr * hi + li * hr + bim_scr[idx, :]
        bre_scr[idx, :] = nr
        bim_scr[idx, :] = ni
        return nr, ni

    hr, hi = lax.fori_loop(0, tc, step, (stre_scr[...], stim_scr[...]))
    stre_scr[...] = hr
    stim_scr[...] = hi
    hre_out[...] = hr
    him_out[...] = hi

    _s5_output_phase(u_ref, cq_ref, dskip_ref, y_ref, bre_scr, bim_scr, tc, False)


def s5_prompt(u, col0, bq, cq, lb_re, lb_im, d_skip, h0_re, h0_im, tc=256):
    t = u.shape[0]
    n_pairs = lb_re.shape[0]
    d = n_pairs * 2 * GROUP_CH
    tc = min(tc, t)
    nq = bq.shape[0]
    whole = lambda *shape: pl.BlockSpec(shape, lambda i: (0,) * len(shape))
    sd = jax.ShapeDtypeStruct
    return pl.pallas_call(
        functools.partial(_s5_prompt_kernel, tc=tc),
        grid=(t // tc,),
        in_specs=[pl.BlockSpec((tc, d), lambda i: (i, col0 // d)),
                  whole(*bq.shape), whole(*cq.shape),
                  whole(n_pairs, LANES), whole(n_pairs, LANES), whole(1, d),
                  whole(n_pairs, LANES), whole(n_pairs, LANES)],
        out_specs=[pl.BlockSpec((tc, d), lambda i: (i, 0)),
                   whole(n_pairs, LANES), whole(n_pairs, LANES)],
        out_shape=[sd((t, d), F32), sd((n_pairs, LANES), F32), sd((n_pairs, LANES), F32)],
        scratch_shapes=[pltpu.VMEM((n_pairs * tc, LANES), F32), pltpu.VMEM((n_pairs * tc, LANES), F32),
                        pltpu.VMEM((n_pairs, LANES), F32), pltpu.VMEM((n_pairs, LANES), F32)],
        compiler_params=_params("arbitrary"),
        name="s5_prompt",
    )(u, bq.astype(BF16), cq.astype(BF16), lb_re, lb_im, d_skip.reshape(1, d), h0_re, h0_im)


def _s5_sample_kernel(u_ref, bq_ref, cq_ref, lbre_ref, lbim_ref, dskip_ref, h0re_ref, h0im_ref,
                      y_ref, hre_out, him_out, bre_scr, bim_scr, *, nt, nb):
    rows = nt * nb
    n_pairs = lbre_ref.shape[0]
    _s5_input_phase(u_ref, bq_ref, bre_scr, bim_scr, rows, True)

    def pair_body(pr, carry):
        lr = lbre_ref[pl.ds(pr, 1), :]
        li = lbim_ref[pl.ds(pr, 1), :]
        hr = h0re_ref[pr]
        hi = h0im_ref[pr]
        for t in range(nt):
            idx = pl.ds(pl.multiple_of(pr * rows + t * nb, 8), nb)
            nr = lr * hr - li * hi + bre_scr[idx, :]
            ni = lr * hi + li * hr + bim_scr[idx, :]
            bre_scr[idx, :] = nr
            bim_scr[idx, :] = ni
            hr, hi = nr, ni
        hre_out[pr] = hr
        him_out[pr] = hi
        return carry

    lax.fori_loop(0, n_pairs, pair_body, 0)
    _s5_output_phase(u_ref, cq_ref, dskip_ref, y_ref, bre_scr, bim_scr, rows, False)


def s5_sample(u, bq, cq, lb_re, lb_im, d_skip, h0_re, h0_im, nt, nb):
    rows, d = u.shape
    n_pairs = lb_re.shape[0]
    sd = jax.ShapeDtypeStruct
    return pl.pallas_call(
        functools.partial(_s5_sample_kernel, nt=nt, nb=nb),
        out_shape=[sd((rows, d), F32), sd((n_pairs, nb, LANES), F32), sd((n_pairs, nb, LANES), F32)],
        scratch_shapes=[pltpu.VMEM((n_pairs * rows, LANES), F32), pltpu.VMEM((n_pairs * rows, LANES), F32)],
        compiler_params=pltpu.CompilerParams(vmem_limit_bytes=VMEM_LIMIT),
        name="s5_sample",
    )(u, bq, cq.astype(BF16), lb_re, lb_im, d_skip.reshape(1, d), h0_re, h0_im)


def _rms(x, g):
    ms = jnp.mean(x * x, axis=-1, keepdims=True)
    return x * lax.rsqrt(ms + EPS) * g


def _mix_kernel(att_ref, y_ref, ga_ref, gs_ref, wglu_ref, bglu_ref, gatt_ref, gssm_ref,
                attn_ref, ssmn_ref):
    attn_ref[...] = (_rms(att_ref[...], gatt_ref[...]) * jax.nn.silu(ga_ref[...])).astype(attn_ref.dtype)
    y = jax.nn.gelu(y_ref[...])
    y = y * jax.nn.sigmoid(_dot(y.astype(BF16), wglu_ref[...]) + bglu_ref[...])
    ssmn_ref[...] = (_rms(y, gssm_ref[...]) * jax.nn.silu(gs_ref[...])).astype(ssmn_ref.dtype)


def mix_post(att, y, gug, w_glu, b_glu, g_att, g_ssm, tm=256):
    m, d = att.shape
    tm = min(tm, m)
    row = lambda c: pl.BlockSpec((tm, d), lambda i: (i, c))
    vec = pl.BlockSpec((1, d), lambda i: (0, 0))
    return pl.pallas_call(
        _mix_kernel,
        grid=(m // tm,),
        in_specs=[row(0), row(0), row(0), row(2),
                  pl.BlockSpec((d, d), lambda i: (0, 0)), vec, vec, vec],
        out_specs=[row(0), row(0)],
        out_shape=[jax.ShapeDtypeStruct((m, d), BF16), jax.ShapeDtypeStruct((m, d), BF16)],
        compiler_params=_params("parallel"),
        name="mix_post",
    )(att, y, gug, gug, w_glu, b_glu.reshape(1, d), g_att.reshape(1, d), g_ssm.reshape(1, d))


def _out_kernel(x_ref, a_ref, s_ref, wa_ref, ws_ref, o_ref):
    o_ref[...] = x_ref[...] + (_dot(a_ref[...], wa_ref[...]) + _dot(s_ref[...], ws_ref[...]))


def out_project(x, att_n, ssm_n, w_out, tm=1024, tn=1024):
    m, dm = x.shape
    dh = att_n.shape[1]
    tm = min(tm, m)
    tn = min(tn, dm)
    return pl.pallas_call(
        _out_kernel,
        grid=(m // tm, dm // tn),
        in_specs=[pl.BlockSpec((tm, tn), lambda i, j: (i, j)),
                  pl.BlockSpec((tm, dh), lambda i, j: (i, 0)),
                  pl.BlockSpec((tm, dh), lambda i, j: (i, 0)),
                  pl.BlockSpec((dh, tn), lambda i, j: (0, j)),
                  pl.BlockSpec((dh, tn), lambda i, j: (1, j))],
        out_specs=pl.BlockSpec((tm, tn), lambda i, j: (i, j)),
        out_shape=jax.ShapeDtypeStruct((m, dm), F32),
        compiler_params=_params("parallel", "arbitrary"),
        name="out_proj",
    )(x, att_n, ssm_n, w_out, w_out)


def _ple_kernel(h_ref, xn_ref, p_ref, wg_ref, wp_ref, o_ref):
    gate = jax.nn.sigmoid(_dot(xn_ref[...], wg_ref[...]))
    ple = _dot(p_ref[...].astype(BF16), wp_ref[...])
    o_ref[...] = h_ref[...] + ple * gate


def ple_gate(h, xn, p, w_gate, w_ple, tm=1024, tn=512):
    m, dm = h.shape
    dp = p.shape[1]
    tm = min(tm, m)
    tn = min(tn, dm)
    return pl.pallas_call(
        _ple_kernel,
        grid=(m // tm, dm // tn),
        in_specs=[pl.BlockSpec((tm, tn), lambda i, j: (i, j)),
                  pl.BlockSpec((tm, dm), lambda i, j: (i, 0)),
                  pl.BlockSpec((tm, dp), lambda i, j: (i, 0)),
                  pl.BlockSpec((dm, tn), lambda i, j: (0, j)),
                  pl.BlockSpec((dp, tn), lambda i, j: (0, j))],
        out_specs=pl.BlockSpec((tm, tn), lambda i, j: (i, j)),
        out_shape=jax.ShapeDtypeStruct((m, dm), F32),
        compiler_params=_params("parallel", "arbitrary"),
        name="ple_gate",
    )(h, xn, p, w_gate, w_ple)


def _pair_layout(x):
    return x.reshape(*x.shape[:-2], x.shape[-2] // 2, 2 * x.shape[-1])


def kernel(x_prompt, x_sample, cache_k, cache_v, state_ssm_re, state_ssm_im, page_table, p_prompt, p_sample, g_in, w_in, att_bias, a_re, a_im, log_dt, b_re, b_im, c_re, c_im, d_skip, w_glu, b_glu, g_att, g_ssm, w_out, g_ple, w_ple, w_ple_gate, g_final):
    depth = w_in.shape[0]
    assert depth == 1, "single-layer step"
    nb_p, t_p, dm = x_prompt.shape
    assert nb_p == 1, "one prompt sequence"
    nb, nt, _ = x_sample.shape
    n_groups, n_state = a_re.shape[1:]
    d_ssm = n_groups * GROUP_CH
    d_att = dm - d_ssm
    nh = d_att // HEAD_DIM
    page = cache_k.shape[2]
    scale = 1.0 / math.sqrt(HEAD_DIM)

    w_in_b = w_in[0].astype(BF16)
    w_glu_b = w_glu[0].astype(BF16)
    w_out_b = w_out[0].astype(BF16)
    w_ple_b = w_ple[0].astype(BF16)
    w_gate_b = w_ple_gate[0].astype(BF16)

    lb_re, lb_im, bb_re, bb_im = s5_discretize(a_re[0], a_im[0], log_dt[0], b_re[0], b_im[0])
    bq, cq = _s5_block_operands(bb_re, bb_im, c_re[0], c_im[0])
    lb_re_p, lb_im_p = _pair_layout(lb_re), _pair_layout(lb_im)
    dsk = d_skip[0].reshape(-1)

    def in_proj(x2d):
        xn = rmsnorm_rows(x2d, g_in[0], BF16)
        (q,) = project(xn, w_in_b, 0, d_att, [BF16], scale=scale)
        k, k_b = project(xn, w_in_b, d_att, d_att, [F32, BF16])
        v, v_b = project(xn, w_in_b, 2 * d_att, d_att, [F32, BF16])
        (gug,) = project(xn, w_in_b, 3 * d_att, d_att + 2 * d_ssm, [F32])
        return q, k, k_b, v, v_b, gug

    def tail(x2d, p2d, att, y, gug):
        att_n, ssm_n = mix_post(att, y, gug, w_glu_b, b_glu[0], g_att[0], g_ssm[0])
        h2 = out_project(x2d, att_n, ssm_n, w_out_b)
        xn2 = rmsnorm_rows(h2, g_ple[0], BF16)
        h3 = ple_gate(h2, xn2, p2d, w_gate_b, w_ple_b)
        return rmsnorm_rows(h3, g_final, F32)

    xp = x_prompt.reshape(t_p, dm)
    q, k, k_b, v, v_b, gug = in_proj(xp)
    att = attention_prompt(q, k_b, v_b, att_bias[0])
    zero_state = jnp.zeros((n_groups // 2, 2 * n_state), F32)
    y, hre_p, him_p = s5_prompt(gug, d_att, bq, cq, lb_re_p, lb_im_p, dsk, zero_state, zero_state)
    y_prompt = tail(xp, p_prompt[0, 0], att, y, gug).reshape(1, t_p, dm)
    new_k_prompt = k.reshape(1, 1, t_p, nh, HEAD_DIM)
    new_v_prompt = v.reshape(1, 1, t_p, nh, HEAD_DIM)
    new_re_prompt = hre_p.reshape(1, 1, n_groups, n_state)
    new_im_prompt = him_p.reshape(1, 1, n_groups, n_state)

    xs = x_sample.reshape(nb * nt, dm)
    q, k, _, v, _, gug = in_proj(xs)
    att = attention_sample(
        q.reshape(nb, nt, d_att), k.reshape(nb, nt, d_att), v.reshape(nb, nt, d_att),
        cache_k[0].reshape(-1, page, d_att), cache_v[0].reshape(-1, page, d_att),
        page_table, att_bias[0]).reshape(nb * nt, d_att)
    u_tb = gug[:, d_att:d_att + d_ssm].reshape(nb, nt, d_ssm).transpose(1, 0, 2).reshape(nt * nb, d_ssm)
    h0_re = _pair_layout(state_ssm_re[0]).transpose(1, 0, 2)
    h0_im = _pair_layout(state_ssm_im[0]).transpose(1, 0, 2)
    y_tb, hre_s, him_s = s5_sample(u_tb, bq, cq, lb_re_p, lb_im_p, dsk, h0_re, h0_im, nt, nb)
    y = y_tb.reshape(nt, nb, d_ssm).transpose(1, 0, 2).reshape(nb * nt, d_ssm)
    y_sample = tail(xs, p_sample[0].reshape(nb * nt, -1), att, y, gug).reshape(nb, nt, dm)
    new_k_sample = k.reshape(1, nb, nt, nh, HEAD_DIM)
    new_v_sample = v.reshape(1, nb, nt, nh, HEAD_DIM)
    new_re_sample = hre_s.transpose(1, 0, 2).reshape(1, nb, n_groups, n_state)
    new_im_sample = him_s.transpose(1, 0, 2).reshape(1, nb, n_groups, n_state)

    return (y_prompt, y_sample, new_k_prompt, new_v_prompt, new_k_sample, new_v_sample,
            new_re_prompt, new_im_prompt, new_re_sample, new_im_sample)
```

```python
import functools
import math

import jax
import jax.numpy as jnp
from jax import lax
from jax.experimental import pallas as pl
from jax.experimental.pallas import tpu as pltpu

F32 = jnp.float32
BF16 = jnp.bfloat16

HEAD_DIM = 128
GROUP_CH = 16
STATE_N = 64
EPS = 1e-6
LANES = 128
VMEM_LIMIT = 56 * 1024 * 1024
GROUPS_PER_QUAD = LANES // GROUP_CH
PAIRS_PER_QUAD = GROUPS_PER_QUAD // 2
SUBLANES = 8
STRIDE_PAD = 4


def _params(*sem):
    return pltpu.CompilerParams(dimension_semantics=sem, vmem_limit_bytes=VMEM_LIMIT)


def _dot(a, b):
    return jnp.dot(a, b, preferred_element_type=F32)


def _split_bf16(x):
    hi = x.astype(BF16)
    lo = (x - hi.astype(F32)).astype(BF16)
    return hi, lo


def _dot_x3(a, b):
    ah, al = _split_bf16(a)
    bh, bl = _split_bf16(b)
    return _dot(ah, bh) + _dot(al, bh) + _dot(ah, bl)


def _softplus(z):
    return jnp.maximum(z, 0.0) + jnp.log(1.0 + jnp.exp(-jnp.abs(z)))


def _rms_kernel(x_ref, g_ref, o_ref):
    x = x_ref[...]
    ms = jnp.mean(x * x, axis=-1, keepdims=True)
    o_ref[...] = (x * lax.rsqrt(ms + EPS) * g_ref[...]).astype(o_ref.dtype)


def rmsnorm_rows(x, g, out_dtype, tm=256):
    m, d = x.shape
    tm = min(tm, m)
    return pl.pallas_call(
        _rms_kernel,
        grid=(m // tm,),
        in_specs=[pl.BlockSpec((tm, d), lambda i: (i, 0)),
                  pl.BlockSpec((1, d), lambda i: (0, 0))],
        out_specs=pl.BlockSpec((tm, d), lambda i: (i, 0)),
        out_shape=jax.ShapeDtypeStruct((m, d), out_dtype),
        compiler_params=_params("parallel"),
        name="rmsnorm",
    )(x, g.reshape(1, d))


def _proj_kernel(x_ref, w_ref, *o_refs, scale):
    acc = _dot(x_ref[...], w_ref[...])
    if scale != 1.0:
        acc = acc * scale
    for o_ref in o_refs:
        o_ref[...] = acc.astype(o_ref.dtype)


def project(xn, w, col0, n, out_dtypes, scale=1.0, tm=1024, tn=1024):
    m, k = xn.shape
    tm = min(tm, m)
    tn = math.gcd(tn, n, col0)
    off = col0 // tn
    return pl.pallas_call(
        functools.partial(_proj_kernel, scale=scale),
        grid=(m // tm, n // tn),
        in_specs=[pl.BlockSpec((tm, k), lambda i, j: (i, 0)),
                  pl.BlockSpec((k, tn), lambda i, j: (0, off + j))],
        out_specs=[pl.BlockSpec((tm, tn), lambda i, j: (i, j)) for _ in out_dtypes],
        out_shape=[jax.ShapeDtypeStruct((m, n), dt) for dt in out_dtypes],
        compiler_params=_params("parallel", "arbitrary"),
        name="in_proj",
    )(xn, w)


def _tail_matrix(tk, row_sums):
    j = jnp.arange(tk)[:, None]
    s = jnp.arange(tk)[None, :]
    u = (j > s).astype(BF16)
    if row_sums:
        u = jnp.concatenate([u, jnp.ones((tk, LANES), BF16)], axis=1)
    return jnp.concatenate([u, u], axis=0)


def _suffix_sums(sp, uo):
    return _dot(jnp.concatenate(_split_bf16(sp), axis=1), uo)


def _sb_tile(q, k, v, bias, uo, r, mask, tk):
    s = lax.dot_general(q, k, (((1,), (1,)), ((), ())), preferred_element_type=F32)
    z = s + bias
    sp = _softplus(z)
    if mask is not None:
        sp = jnp.where(mask, sp, 0.0)
    t2 = _suffix_sums(sp, uo)
    rb = jnp.concatenate([r] * (tk // LANES), axis=1) if tk > LANES else r
    w = jnp.exp(z - sp - t2[:, :tk] - rb)
    if mask is not None:
        w = jnp.where(mask, w, 0.0)
    if uo.shape[1] > tk:
        rs = t2[:, tk:]
    else:
        rs = jnp.broadcast_to(jnp.sum(sp, axis=1, keepdims=True), r.shape)
    return _dot(w.astype(BF16), v), rs


def _attn_kernel(bias_ref, q_ref, k_ref, v_ref, uo_ref, o_ref, acc_ref, r_ref, *, tq, tk):
    h = pl.program_id(0)
    qi = pl.program_id(1)
    bias = bias_ref[h]
    q = q_ref[...]
    uo = uo_ref[...]
    acc_ref[...] = jnp.zeros_like(acc_ref)
    r_ref[...] = jnp.zeros_like(r_ref)
    nd = tq // tk

    def tile(kt, row0, mask):
        start = pl.multiple_of(kt * tk, tk)
        k = k_ref[pl.ds(start, tk), :]
        v = v_ref[pl.ds(start, tk), :]
        r = r_ref[row0:, :]
        pv, rs = _sb_tile(q[row0:, :], k, v, bias, uo, r, mask, tk)
        acc_ref[row0:, :] += pv
        r_ref[row0:, :] = r + rs

    for d in range(nd - 1, -1, -1):
        row0 = d * tk
        row = lax.broadcasted_iota(jnp.int32, (tq - row0, tk), 0)
        col = lax.broadcasted_iota(jnp.int32, (tq - row0, tk), 1)
        tile(qi * nd + d, row0, col < row)

    def body(i, carry):
        for d in range(nd):
            tile((qi - i) * nd - 1 - d, 0, None)
        return carry

    lax.fori_loop(0, qi, body, 0)
    o_ref[...] = acc_ref[...]


def attention_prompt(q, k, v, bias, tq=1024, tk=256):
    t, d = q.shape
    nh = d // HEAD_DIM
    tq = min(tq, t)
    tk = min(tk, tq)
    return pl.pallas_call(
        functools.partial(_attn_kernel, tq=tq, tk=tk),
        grid=(nh, t // tq),
        in_specs=[pl.BlockSpec(memory_space=pltpu.SMEM),
                  pl.BlockSpec((tq, HEAD_DIM), lambda h, i: (i, h)),
                  pl.BlockSpec((t, HEAD_DIM), lambda h, i: (0, h)),
                  pl.BlockSpec((t, HEAD_DIM), lambda h, i: (0, h)),
                  pl.BlockSpec((2 * tk, tk), lambda h, i: (0, 0))],
        out_specs=pl.BlockSpec((tq, HEAD_DIM), lambda h, i: (i, h)),
        out_shape=jax.ShapeDtypeStruct((t, d), F32),
        scratch_shapes=[pltpu.VMEM((tq, HEAD_DIM), F32), pltpu.VMEM((tq, LANES), F32)],
        compiler_params=_params("parallel", "arbitrary"),
        name="sb_attention_prompt",
    )(bias, q, k, v, _tail_matrix(tk, row_sums=False))


def _dec_attn_kernel(pt_ref, q_ref, bias_ref, knew_ref, vnew_ref, *rest, nt, nh, page, gp):
    del pt_ref
    k_refs, v_refs = rest[:gp], rest[gp:2 * gp]
    uo_ref, o_ref, acc_ref, r_ref, kpad_ref, vpad_ref = rest[2 * gp:]
    j = pl.program_id(1)
    rows = nh * nt
    hpg = SUBLANES // nt
    ng = rows // SUBLANES
    bias = bias_ref[...]
    uo = uo_ref[...]
    sub_head = lax.broadcasted_iota(jnp.int32, (SUBLANES, page), 0) // nt

    def head_rows(ref, g):
        return jnp.concatenate(
            [ref[pl.ds(g * hpg + p, page, stride=nh), :] for p in range(hpg)], axis=0).astype(BF16)

    def process(k_ref, v_ref, mask):
        zs = []
        for g in range(ng):
            qg = q_ref[g * SUBLANES:(g + 1) * SUBLANES, :].astype(BF16)
            s = lax.dot_general(qg, head_rows(k_ref, g), (((1,), (1,)), ((), ())),
                                preferred_element_type=F32)
            z = s[:, :page]
            for p in range(1, hpg):
                z = jnp.where(sub_head == p, s[:, p * page:(p + 1) * page], z)
            zs.append(z)
        z = jnp.concatenate(zs, axis=0) + bias
        sp = _softplus(z)
        if mask is not None:
            sp = jnp.where(mask, sp, 0.0)
        t2 = _suffix_sums(sp, uo)
        r = r_ref[...]
        w = jnp.exp(z - sp - t2[:, :page] - r)
        if mask is not None:
            w = jnp.where(mask, w, 0.0)
        for g in range(ng):
            wg = w[g * SUBLANES:(g + 1) * SUBLANES, :]
            wexp = jnp.concatenate([jnp.where(sub_head == p, wg, 0.0) for p in range(hpg)],
                                   axis=1).astype(BF16)
            acc_ref[g * SUBLANES:(g + 1) * SUBLANES, :] += _dot(wexp, head_rows(v_ref, g))
        r_ref[...] = r + t2[:, page:]

    @pl.when(j == 0)
    def _():
        acc_ref[...] = jnp.zeros_like(acc_ref)
        r_ref[...] = jnp.zeros_like(r_ref)
        kpad_ref[...] = jnp.zeros_like(kpad_ref)
        vpad_ref[...] = jnp.zeros_like(vpad_ref)
        kpad_ref[0:nt * nh, :] = knew_ref[...]
        vpad_ref[0:nt * nh, :] = vnew_ref[...]
        tok = lax.broadcasted_iota(jnp.int32, (rows, page), 0) % nt
        key = lax.broadcasted_iota(jnp.int32, (rows, page), 1)
        process(kpad_ref, vpad_ref, key < tok)

    @pl.when(j > 0)
    def _():
        for g in range(gp):
            process(k_refs[g], v_refs[g], None)

    @pl.when(j == pl.num_programs(1) - 1)
    def _():
        o_ref[...] = acc_ref[...]


def attention_sample(q, k_new, v_new, cache_k, cache_v, page_table, bias, pages_per_step=4):
    b, nt, d = q.shape
    nh = d // HEAD_DIM
    n_pool, page = cache_k.shape[:2]
    n_pages = page_table.shape[1]
    rows = nh * nt
    gp = math.gcd(n_pages, pages_per_step)
    assert SUBLANES % nt == 0 and nh % (SUBLANES // nt) == 0
    q_rows = q.astype(F32).reshape(b, nt, nh, HEAD_DIM).transpose(0, 2, 1, 3).reshape(b, rows, HEAD_DIM)
    bias_rows = jnp.broadcast_to(jnp.repeat(bias, nt)[:, None], (rows, LANES)).astype(F32)
    cache_k = cache_k.reshape(n_pool, page * nh, HEAD_DIM)
    cache_v = cache_v.reshape(n_pool, page * nh, HEAD_DIM)
    k_new = k_new.reshape(b, nt * nh, HEAD_DIM)
    v_new = v_new.reshape(b, nt * nh, HEAD_DIM)

    def page_spec(g):
        def index_map(i, j, pt):
            return (pt[i, n_pages - 1 - (jnp.maximum(j, 1) - 1) * gp - g], 0, 0)
        return pl.BlockSpec((None, page * nh, HEAD_DIM), index_map)

    per_seq = lambda r: pl.BlockSpec((None, r, HEAD_DIM), lambda i, j, pt: (i, 0, 0))
    grid_spec = pltpu.PrefetchScalarGridSpec(
        num_scalar_prefetch=1,
        grid=(b, n_pages // gp + 1),
        in_specs=[per_seq(rows),
                  pl.BlockSpec((rows, LANES), lambda i, j, pt: (0, 0)),
                  per_seq(nt * nh), per_seq(nt * nh),
                  *[page_spec(g) for g in range(gp)], *[page_spec(g) for g in range(gp)],
                  pl.BlockSpec((2 * page, page + LANES), lambda i, j, pt: (0, 0))],
        out_specs=per_seq(rows),
        scratch_shapes=[pltpu.VMEM((rows, HEAD_DIM), F32), pltpu.VMEM((rows, LANES), F32),
                        pltpu.VMEM((page * nh, HEAD_DIM), F32), pltpu.VMEM((page * nh, HEAD_DIM), F32)],
    )
    out = pl.pallas_call(
        functools.partial(_dec_attn_kernel, nt=nt, nh=nh, page=page, gp=gp),
        grid_spec=grid_spec,
        out_shape=jax.ShapeDtypeStruct((b, rows, HEAD_DIM), F32),
        compiler_params=_params("parallel", "arbitrary"),
        name="sb_attention_sample",
    )(page_table, q_rows, bias_rows, k_new, v_new, *([cache_k] * gp), *([cache_v] * gp),
      _tail_matrix(page, row_sums=True))
    return out.reshape(b, nh, nt, HEAD_DIM).transpose(0, 2, 1, 3).reshape(b, nt, d)


def _s5_discretize_kernel(are_ref, aim_ref, ldt_ref, bre_ref, bim_ref,
                          lbre_ref, lbim_ref, bbre_ref, bbim_ref):
    a_re = are_ref[...]
    a_im = aim_ref[...]
    dt = jnp.exp(ldt_ref[...])
    mag = jnp.exp(a_re * dt)
    ang = a_im * dt
    lb_re = mag * jnp.cos(ang)
    lb_im = mag * jnp.sin(ang)
    den = a_re * a_re + a_im * a_im
    num_re = lb_re - 1.0
    f_re = (num_re * a_re + lb_im * a_im) / den
    f_im = (lb_im * a_re - num_re * a_im) / den
    lbre_ref[...] = lb_re
    lbim_ref[...] = lb_im
    b_re = bre_ref[...]
    b_im = bim_ref[...]
    bbre_ref[...] = f_re * b_re - f_im * b_im
    bbim_ref[...] = f_re * b_im + f_im * b_re


def s5_discretize(a_re, a_im, log_dt, b_re, b_im):
    g, n = a_re.shape
    p = b_re.shape[-1]
    rep = lambda x: jnp.repeat(x, p, axis=0)
    rows = lambda x: x.transpose(0, 2, 1).reshape(g * p, n)
    sd = jax.ShapeDtypeStruct((g * p, n), F32)
    lb_re, lb_im, bb_re, bb_im = pl.pallas_call(
        _s5_discretize_kernel,
        out_shape=[sd, sd, sd, sd],
        name="s5_discretize",
    )(rep(a_re), rep(a_im), rep(jnp.broadcast_to(log_dt[:, None], (g, n))), rows(b_re), rows(b_im))
    return lb_re[::p], lb_im[::p], bb_re.reshape(g, p, n), bb_im.reshape(g, p, n)


def _s5_block_operands(bb_re, bb_im, c_re, c_im):
    g, p, n = bb_re.shape
    nq = g // GROUPS_PER_QUAD
    eye = jnp.eye(GROUPS_PER_QUAD, dtype=F32)

    def b_side(x):
        x = x.reshape(nq, GROUPS_PER_QUAD, p, n)
        return (x[:, :, :, None, :] * eye[None, :, None, :, None]).reshape(
            nq, GROUPS_PER_QUAD * p, GROUPS_PER_QUAD * n)

    def c_side(x):
        x = x.reshape(nq, GROUPS_PER_QUAD, p, n).transpose(0, 1, 3, 2)
        return (x[:, :, :, None, :] * eye[None, :, None, :, None]).reshape(
            nq, GROUPS_PER_QUAD * n, GROUPS_PER_QUAD * p)

    bq = jnp.concatenate([b_side(bb_re), b_side(bb_im)], axis=2)
    cq = jnp.concatenate([c_side(c_re), -c_side(c_im)], axis=1)
    return bq, cq


def _s5_input_phase(u_ref, bq_ref, bre_scr, bim_scr, rows, pitch, precise):
    nq = bq_ref.shape[0]
    half = PAIRS_PER_QUAD * LANES
    for qd in range(nq):
        uq = u_ref[:, qd * LANES:(qd + 1) * LANES]
        if precise:
            bu = _dot_x3(uq, bq_ref[qd])
        else:
            bu = _dot(uq.astype(BF16), bq_ref[qd])
        for jj in range(PAIRS_PER_QUAD):
            r0 = (PAIRS_PER_QUAD * qd + jj) * pitch
            bre_scr[r0:r0 + rows, :] = bu[:, jj * LANES:(jj + 1) * LANES]
            bim_scr[r0:r0 + rows, :] = bu[:, half + jj * LANES:half + (jj + 1) * LANES]


def _s5_output_phase(u_ref, cq_ref, dskip_ref, y_ref, hre_scr, him_scr, rows, pitch, precise):
    nq = cq_ref.shape[0]
    for qd in range(nq):
        parts = []
        for scr in (hre_scr, him_scr):
            for jj in range(PAIRS_PER_QUAD):
                r0 = (PAIRS_PER_QUAD * qd + jj) * pitch
                parts.append(scr[r0:r0 + rows, :])
        hcat = jnp.concatenate(parts, axis=1)
        if precise:
            yq = _dot_x3(hcat, cq_ref[qd])
        else:
            yq = _dot(hcat.astype(BF16), cq_ref[qd])
        sl = slice(qd * LANES, (qd + 1) * LANES)
        y_ref[:, sl] = yq + dskip_ref[:, sl] * u_ref[:, sl]


def _s5_prompt_kernel(u_ref, bq_ref, cq_ref, lbre_ref, lbim_ref, dskip_ref, h0re_ref, h0im_ref,
                      y_ref, hre_out, him_out, bre_scr, bim_scr, stre_scr, stim_scr, *, tc):
    i = pl.program_id(0)
    n_pairs = lbre_ref.shape[0]

    @pl.when(i == 0)
    def _():
        stre_scr[...] = h0re_ref[...]
        stim_scr[...] = h0im_ref[...]

    pitch = tc + STRIDE_PAD
    _s5_input_phase(u_ref, bq_ref, bre_scr, bim_scr, tc, pitch, False)

    lr = lbre_ref[...]
    li = lbim_ref[...]

    def step(t, carry):
        hr, hi = carry
        idx = pl.ds(t, n_pairs, stride=pitch)
        nr = lr * hr - li * hi + bre_scr[idx, :]
        ni = lr * hi + li * hr + bim_scr[idx, :]
        bre_scr[idx, :] = nr
        bim_scr[idx, :] = ni
        return nr, ni

    hr, hi = lax.fori_loop(0, tc, step, (stre_scr[...], stim_scr[...]))
    stre_scr[...] = hr
    stim_scr[...] = hi
    hre_out[...] = hr
    him_out[...] = hi

    _s5_output_phase(u_ref, cq_ref, dskip_ref, y_ref, bre_scr, bim_scr, tc, pitch, False)


def s5_prompt(u, col0, bq, cq, lb_re, lb_im, d_skip, h0_re, h0_im, tc=256):
    t = u.shape[0]
    n_pairs = lb_re.shape[0]
    d = n_pairs * 2 * GROUP_CH
    tc = min(tc, t)
    nq = bq.shape[0]
    whole = lambda *shape: pl.BlockSpec(shape, lambda i: (0,) * len(shape))
    sd = jax.ShapeDtypeStruct
    return pl.pallas_call(
        functools.partial(_s5_prompt_kernel, tc=tc),
        grid=(t // tc,),
        in_specs=[pl.BlockSpec((tc, d), lambda i: (i, col0 // d)),
                  whole(*bq.shape), whole(*cq.shape),
                  whole(n_pairs, LANES), whole(n_pairs, LANES), whole(1, d),
                  whole(n_pairs, LANES), whole(n_pairs, LANES)],
        out_specs=[pl.BlockSpec((tc, d), lambda i: (i, 0)),
                   whole(n_pairs, LANES), whole(n_pairs, LANES)],
        out_shape=[sd((t, d), F32), sd((n_pairs, LANES), F32), sd((n_pairs, LANES), F32)],
        scratch_shapes=[pltpu.VMEM((n_pairs * (tc + STRIDE_PAD), LANES), F32),
                        pltpu.VMEM((n_pairs * (tc + STRIDE_PAD), LANES), F32),
                        pltpu.VMEM((n_pairs, LANES), F32), pltpu.VMEM((n_pairs, LANES), F32)],
        compiler_params=_params("arbitrary"),
        name="s5_prompt",
    )(u, bq.astype(BF16), cq.astype(BF16), lb_re, lb_im, d_skip.reshape(1, d), h0_re, h0_im)


def _s5_sample_kernel(u_ref, bq_ref, cq_ref, lbre_ref, lbim_ref, dskip_ref, h0re_ref, h0im_ref,
                      y_ref, hre_out, him_out, bre_scr, bim_scr, *, nt, nb):
    rows = nt * nb
    n_pairs = lbre_ref.shape[0]
    _s5_input_phase(u_ref, bq_ref, bre_scr, bim_scr, rows, rows, True)

    def pair_body(pr, carry):
        lr = lbre_ref[pl.ds(pr, 1), :]
        li = lbim_ref[pl.ds(pr, 1), :]
        hr = h0re_ref[pr]
        hi = h0im_ref[pr]
        for t in range(nt):
            idx = pl.ds(pl.multiple_of(pr * rows + t * nb, 8), nb)
            nr = lr * hr - li * hi + bre_scr[idx, :]
            ni = lr * hi + li * hr + bim_scr[idx, :]
            bre_scr[idx, :] = nr
            bim_scr[idx, :] = ni
            hr, hi = nr, ni
        hre_out[pr] = hr
        him_out[pr] = hi
        return carry

    lax.fori_loop(0, n_pairs, pair_body, 0)
    _s5_output_phase(u_ref, cq_ref, dskip_ref, y_ref, bre_scr, bim_scr, rows, rows, False)


def s5_sample(u, bq, cq, lb_re, lb_im, d_skip, h0_re, h0_im, nt, nb):
    rows, d = u.shape
    n_pairs = lb_re.shape[0]
    sd = jax.ShapeDtypeStruct
    return pl.pallas_call(
        functools.partial(_s5_sample_kernel, nt=nt, nb=nb),
        out_shape=[sd((rows, d), F32), sd((n_pairs, nb, LANES), F32), sd((n_pairs, nb, LANES), F32)],
        scratch_shapes=[pltpu.VMEM((n_pairs * rows, LANES), F32), pltpu.VMEM((n_pairs * rows, LANES), F32)],
        compiler_params=pltpu.CompilerParams(vmem_limit_bytes=VMEM_LIMIT),
        name="s5_sample",
    )(u, bq, cq.astype(BF16), lb_re, lb_im, d_skip.reshape(1, d), h0_re, h0_im)


def _rms(x, g):
    ms = jnp.mean(x * x, axis=-1, keepdims=True)
    return x * lax.rsqrt(ms + EPS) * g


def _mix_kernel(att_ref, y_ref, ga_ref, gs_ref, wglu_ref, bglu_ref, gatt_ref, gssm_ref,
                attn_ref, ssmn_ref):
    attn_ref[...] = (_rms(att_ref[...], gatt_ref[...]) * jax.nn.silu(ga_ref[...])).astype(attn_ref.dtype)
    y = jax.nn.gelu(y_ref[...])
    y = y * jax.nn.sigmoid(_dot(y.astype(BF16), wglu_ref[...]) + bglu_ref[...])
    ssmn_ref[...] = (_rms(y, gssm_ref[...]) * jax.nn.silu(gs_ref[...])).astype(ssmn_ref.dtype)


def mix_post(att, y, gug, w_glu, b_glu, g_att, g_ssm, tm=256):
    m, d = att.shape
    tm = min(tm, m)
    row = lambda c: pl.BlockSpec((tm, d), lambda i: (i, c))
    vec = pl.BlockSpec((1, d), lambda i: (0, 0))
    return pl.pallas_call(
        _mix_kernel,
        grid=(m // tm,),
        in_specs=[row(0), row(0), row(0), row(2),
                  pl.BlockSpec((d, d), lambda i: (0, 0)), vec, vec, vec],
        out_specs=[row(0), row(0)],
        out_shape=[jax.ShapeDtypeStruct((m, d), BF16), jax.ShapeDtypeStruct((m, d), BF16)],
        compiler_params=_params("parallel"),
        name="mix_post",
    )(att, y, gug, gug, w_glu, b_glu.reshape(1, d), g_att.reshape(1, d), g_ssm.reshape(1, d))


def _out_kernel(x_ref, a_ref, s_ref, wa_ref, ws_ref, o_ref):
    o_ref[...] = x_ref[...] + (_dot(a_ref[...], wa_ref[...]) + _dot(s_ref[...], ws_ref[...]))


def out_project(x, att_n, ssm_n, w_out, tm=1024, tn=1024):
    m, dm = x.shape
    dh = att_n.shape[1]
    tm = min(tm, m)
    tn = min(tn, dm)
    return pl.pallas_call(
        _out_kernel,
        grid=(m // tm, dm // tn),
        in_specs=[pl.BlockSpec((tm, tn), lambda i, j: (i, j)),
                  pl.BlockSpec((tm, dh), lambda i, j: (i, 0)),
                  pl.BlockSpec((tm, dh), lambda i, j: (i, 0)),
                  pl.BlockSpec((dh, tn), lambda i, j: (0, j)),
                  pl.BlockSpec((dh, tn), lambda i, j: (1, j))],
        out_specs=pl.BlockSpec((tm, tn), lambda i, j: (i, j)),
        out_shape=jax.ShapeDtypeStruct((m, dm), F32),
        compiler_params=_params("parallel", "arbitrary"),
        name="out_proj",
    )(x, att_n, ssm_n, w_out, w_out)


def _ple_kernel(h_ref, xn_ref, p_ref, wg_ref, wp_ref, o_ref):
    gate = jax.nn.sigmoid(_dot(xn_ref[...], wg_ref[...]))
    ple = _dot(p_ref[...].astype(BF16), wp_ref[...])
    o_ref[...] = h_ref[...] + ple * gate


def ple_gate(h, xn, p, w_gate, w_ple, tm=1024, tn=512):
    m, dm = h.shape
    dp = p.shape[1]
    tm = min(tm, m)
    tn = min(tn, dm)
    return pl.pallas_call(
        _ple_kernel,
        grid=(m // tm, dm // tn),
        in_specs=[pl.BlockSpec((tm, tn), lambda i, j: (i, j)),
                  pl.BlockSpec((tm, dm), lambda i, j: (i, 0)),
                  pl.BlockSpec((tm, dp), lambda i, j: (i, 0)),
                  pl.BlockSpec((dm, tn), lambda i, j: (0, j)),
                  pl.BlockSpec((dp, tn), lambda i, j: (0, j))],
        out_specs=pl.BlockSpec((tm, tn), lambda i, j: (i, j)),
        out_shape=jax.ShapeDtypeStruct((m, dm), F32),
        compiler_params=_params("parallel", "arbitrary"),
        name="ple_gate",
    )(h, xn, p, w_gate, w_ple)


def _pair_layout(x):
    return x.reshape(*x.shape[:-2], x.shape[-2] // 2, 2 * x.shape[-1])


def kernel(x_prompt, x_sample, cache_k, cache_v, state_ssm_re, state_ssm_im, page_table, p_prompt, p_sample, g_in, w_in, att_bias, a_re, a_im, log_dt, b_re, b_im, c_re, c_im, d_skip, w_glu, b_glu, g_att, g_ssm, w_out, g_ple, w_ple, w_ple_gate, g_final):
    depth = w_in.shape[0]
    assert depth == 1, "single-layer step"
    nb_p, t_p, dm = x_prompt.shape
    assert nb_p == 1, "one prompt sequence"
    nb, nt, _ = x_sample.shape
    n_groups, n_state = a_re.shape[1:]
    d_ssm = n_groups * GROUP_CH
    d_att = dm - d_ssm
    nh = d_att // HEAD_DIM
    page = cache_k.shape[2]
    scale = 1.0 / math.sqrt(HEAD_DIM)

    w_in_b = w_in[0].astype(BF16)
    w_glu_b = w_glu[0].astype(BF16)
    w_out_b = w_out[0].astype(BF16)
    w_ple_b = w_ple[0].astype(BF16)
    w_gate_b = w_ple_gate[0].astype(BF16)

    lb_re, lb_im, bb_re, bb_im = s5_discretize(a_re[0], a_im[0], log_dt[0], b_re[0], b_im[0])
    bq, cq = _s5_block_operands(bb_re, bb_im, c_re[0], c_im[0])
    lb_re_p, lb_im_p = _pair_layout(lb_re), _pair_layout(lb_im)
    dsk = d_skip[0].reshape(-1)

    def in_proj(x2d):
        xn = rmsnorm_rows(x2d, g_in[0], BF16)
        (q,) = project(xn, w_in_b, 0, d_att, [BF16], scale=scale)
        k, k_b = project(xn, w_in_b, d_att, d_att, [F32, BF16])
        v, v_b = project(xn, w_in_b, 2 * d_att, d_att, [F32, BF16])
        (gug,) = project(xn, w_in_b, 3 * d_att, d_att + 2 * d_ssm, [F32])
        return q, k, k_b, v, v_b, gug

    def tail(x2d, p2d, att, y, gug):
        att_n, ssm_n = mix_post(att, y, gug, w_glu_b, b_glu[0], g_att[0], g_ssm[0])
        h2 = out_project(x2d, att_n, ssm_n, w_out_b)
        xn2 = rmsnorm_rows(h2, g_ple[0], BF16)
        h3 = ple_gate(h2, xn2, p2d, w_gate_b, w_ple_b)
        return rmsnorm_rows(h3, g_final, F32)

    xp = x_prompt.reshape(t_p, dm)
    q, k, k_b, v, v_b, gug = in_proj(xp)
    att = attention_prompt(q, k_b, v_b, att_bias[0])
    zero_state = jnp.zeros((n_groups // 2, 2 * n_state), F32)
    y, hre_p, him_p = s5_prompt(gug, d_att, bq, cq, lb_re_p, lb_im_p, dsk, zero_state, zero_state)
    y_prompt = tail(xp, p_prompt[0, 0], att, y, gug).reshape(1, t_p, dm)
    new_k_prompt = k.reshape(1, 1, t_p, nh, HEAD_DIM)
    new_v_prompt = v.reshape(1, 1, t_p, nh, HEAD_DIM)
    new_re_prompt = hre_p.reshape(1, 1, n_groups, n_state)
    new_im_prompt = him_p.reshape(1, 1, n_groups, n_state)

    xs = x_sample.reshape(nb * nt, dm)
    q, k, _, v, _, gug = in_proj(xs)
    att = attention_sample(
        q.reshape(nb, nt, d_att), k.reshape(nb, nt, d_att), v.reshape(nb, nt, d_att),
        cache_k[0], cache_v[0], page_table, att_bias[0]).reshape(nb * nt, d_att)
    u_tb = gug[:, d_att:d_att + d_ssm].reshape(nb, nt, d_ssm).transpose(1, 0, 2).reshape(nt * nb, d_ssm)
    h0_re = _pair_layout(state_ssm_re[0]).transpose(1, 0, 2)
    h0_im = _pair_layout(state_ssm_im[0]).transpose(1, 0, 2)
    y_tb, hre_s, him_s = s5_sample(u_tb, bq, cq, lb_re_p, lb_im_p, dsk, h0_re, h0_im, nt, nb)
    y = y_tb.reshape(nt, nb, d_ssm).transpose(1, 0, 2).reshape(nb * nt, d_ssm)
    y_sample = tail(xs, p_sample[0].reshape(nb * nt, -1), att, y, gug).reshape(nb, nt, dm)
    new_k_sample = k.reshape(1, nb, nt, nh, HEAD_DIM)
    new_v_sample = v.reshape(1, nb, nt, nh, HEAD_DIM)
    new_re_sample = hre_s.transpose(1, 0, 2).reshape(1, nb, n_groups, n_state)
    new_im_sample = him_s.transpose(1, 0, 2).reshape(1, nb, n_groups, n_state)

    return (y_prompt, y_sample, new_k_prompt, new_v_prompt, new_k_sample, new_v_sample,
            new_re_prompt, new_im_prompt, new_re_sample, new_im_sample)
```

```python
import functools
import math

import jax
import jax.numpy as jnp
from jax import lax
from jax.experimental import pallas as pl
from jax.experimental.pallas import tpu as pltpu

F32 = jnp.float32
BF16 = jnp.bfloat16

HEAD_DIM = 128
GROUP_CH = 16
STATE_N = 64
EPS = 1e-6
LANES = 128
VMEM_LIMIT = 56 * 1024 * 1024
GROUPS_PER_QUAD = LANES // GROUP_CH
PAIRS_PER_QUAD = GROUPS_PER_QUAD // 2
SUBLANES = 8
HEAD_BLOCK = SUBLANES
STRIDE_PAD = 4


def _params(*sem):
    return pltpu.CompilerParams(dimension_semantics=sem, vmem_limit_bytes=VMEM_LIMIT)


def _dot(a, b):
    return jnp.dot(a, b, preferred_element_type=F32)


def _split_bf16(x):
    hi = x.astype(BF16)
    lo = (x - hi.astype(F32)).astype(BF16)
    return hi, lo


def _dot_x3(a, b):
    ah, al = _split_bf16(a)
    bh, bl = _split_bf16(b)
    return _dot(ah, bh) + _dot(al, bh) + _dot(ah, bl)


def _softplus(z):
    return jnp.maximum(z, 0.0) + jnp.log(1.0 + jnp.exp2(jnp.abs(z) * -math.log2(math.e)))


def _rms_kernel(x_ref, g_ref, o_ref):
    x = x_ref[...]
    ms = jnp.mean(x * x, axis=-1, keepdims=True)
    o_ref[...] = (x * lax.rsqrt(ms + EPS) * g_ref[...]).astype(o_ref.dtype)


def rmsnorm_rows(x, g, out_dtype, tm=256):
    m, d = x.shape
    tm = min(tm, m)
    return pl.pallas_call(
        _rms_kernel,
        grid=(m // tm,),
        in_specs=[pl.BlockSpec((tm, d), lambda i: (i, 0)),
                  pl.BlockSpec((1, d), lambda i: (0, 0))],
        out_specs=pl.BlockSpec((tm, d), lambda i: (i, 0)),
        out_shape=jax.ShapeDtypeStruct((m, d), out_dtype),
        compiler_params=_params("parallel"),
        name="rmsnorm",
    )(x, g.reshape(1, d))


def _proj_kernel(x_ref, w_ref, *o_refs, scale):
    acc = _dot(x_ref[...], w_ref[...])
    if scale != 1.0:
        acc = acc * scale
    for o_ref in o_refs:
        o_ref[...] = acc.astype(o_ref.dtype)


def project(xn, w, col0, n, out_dtypes, scale=1.0, tm=1024, tn=1024):
    m, k = xn.shape
    tm = min(tm, m)
    tn = math.gcd(tn, n, col0)
    off = col0 // tn
    return pl.pallas_call(
        functools.partial(_proj_kernel, scale=scale),
        grid=(m // tm, n // tn),
        in_specs=[pl.BlockSpec((tm, k), lambda i, j: (i, 0)),
                  pl.BlockSpec((k, tn), lambda i, j: (0, off + j))],
        out_specs=[pl.BlockSpec((tm, tn), lambda i, j: (i, j)) for _ in out_dtypes],
        out_shape=[jax.ShapeDtypeStruct((m, n), dt) for dt in out_dtypes],
        compiler_params=_params("parallel", "arbitrary"),
        name="in_proj",
    )(xn, w)


def _tail_matrix(tk, row_sums):
    j = jnp.arange(tk)[:, None]
    s = jnp.arange(tk)[None, :]
    u = (j > s).astype(BF16)
    if row_sums:
        u = jnp.concatenate([u, jnp.ones((tk, LANES), BF16)], axis=1)
    return jnp.concatenate([u, u], axis=0)


def _suffix_sums(sp, uo):
    return _dot(jnp.concatenate(_split_bf16(sp), axis=1), uo)


def _sb_tile(q, k, v, bias, uo, r, mask, tk):
    s = lax.dot_general(q, k, (((1,), (1,)), ((), ())), preferred_element_type=F32)
    z = s + bias
    sp = _softplus(z)
    if mask is not None:
        sp = jnp.where(mask, sp, 0.0)
    t2 = _suffix_sums(sp, uo)
    rb = jnp.concatenate([r] * (tk // LANES), axis=1) if tk > LANES else r
    w = jnp.exp(z - sp - t2[:, :tk] - rb)
    if mask is not None:
        w = jnp.where(mask, w, 0.0)
    if uo.shape[1] > tk:
        rs = t2[:, tk:]
    else:
        rs = jnp.broadcast_to(jnp.sum(sp, axis=1, keepdims=True), r.shape)
    return _dot(w.astype(BF16), v), rs


def _attn_kernel(bias_ref, q_ref, k_ref, v_ref, uo_ref, o_ref, acc_ref, r_ref, *, tq, tk):
    h = pl.program_id(0)
    qi = pl.program_id(1)
    bias = bias_ref[h]
    q = q_ref[...]
    uo = uo_ref[...]
    acc_ref[...] = jnp.zeros_like(acc_ref)
    r_ref[...] = jnp.zeros_like(r_ref)
    nd = tq // tk

    def tile(kt, row0, mask):
        start = pl.multiple_of(kt * tk, tk)
        k = k_ref[pl.ds(start, tk), :]
        v = v_ref[pl.ds(start, tk), :]
        r = r_ref[row0:, :]
        pv, rs = _sb_tile(q[row0:, :], k, v, bias, uo, r, mask, tk)
        acc_ref[row0:, :] += pv
        r_ref[row0:, :] = r + rs

    for d in range(nd - 1, -1, -1):
        row0 = d * tk
        row = lax.broadcasted_iota(jnp.int32, (tq - row0, tk), 0)
        col = lax.broadcasted_iota(jnp.int32, (tq - row0, tk), 1)
        tile(qi * nd + d, row0, col < row)

    def body(i, carry):
        for d in range(nd):
            tile((qi - i) * nd - 1 - d, 0, None)
        return carry

    lax.fori_loop(0, qi, body, 0)
    o_ref[...] = acc_ref[...]


def attention_prompt(q, k, v, bias, tq=1024, tk=256):
    t, d = q.shape
    nh = d // HEAD_DIM
    tq = min(tq, t)
    tk = min(tk, tq)
    return pl.pallas_call(
        functools.partial(_attn_kernel, tq=tq, tk=tk),
        grid=(nh, t // tq),
        in_specs=[pl.BlockSpec(memory_space=pltpu.SMEM),
                  pl.BlockSpec((tq, HEAD_DIM), lambda h, i: (i, h)),
                  pl.BlockSpec((t, HEAD_DIM), lambda h, i: (0, h)),
                  pl.BlockSpec((t, HEAD_DIM), lambda h, i: (0, h)),
                  pl.BlockSpec((2 * tk, tk), lambda h, i: (0, 0))],
        out_specs=pl.BlockSpec((tq, HEAD_DIM), lambda h, i: (i, h)),
        out_shape=jax.ShapeDtypeStruct((t, d), F32),
        scratch_shapes=[pltpu.VMEM((tq, HEAD_DIM), F32), pltpu.VMEM((tq, LANES), F32)],
        compiler_params=_params("parallel", "arbitrary"),
        name="sb_attention_prompt",
    )(bias, q, k, v, _tail_matrix(tk, row_sums=False))


def _dec_attn_kernel(pt_ref, q_ref, bias_ref, knew_ref, vnew_ref, *rest, nt, nh, page, gp):
    del pt_ref
    nblk = nh // HEAD_BLOCK
    npb = gp * nblk
    flat = lambda ref: ref.reshape(page * HEAD_BLOCK, HEAD_DIM)
    k_pages = [[flat(r) for r in rest[g * nblk:(g + 1) * nblk]] for g in range(gp)]
    v_pages = [[flat(r) for r in rest[npb + g * nblk:npb + (g + 1) * nblk]] for g in range(gp)]
    uo_ref, o_ref, acc_ref, r_ref, kpad_ref, vpad_ref = rest[2 * npb:]
    j = pl.program_id(1)
    rows = nh * nt
    hpg = SUBLANES // nt
    ng = rows // SUBLANES
    bias = bias_ref[...]
    uo = uo_ref[...]
    sub_head = lax.broadcasted_iota(jnp.int32, (SUBLANES, page), 0) // nt

    def head_rows(blocks, g):
        parts = []
        for p in range(hpg):
            blk, hb = divmod(g * hpg + p, HEAD_BLOCK)
            parts.append(blocks[blk][pl.ds(hb, page, stride=HEAD_BLOCK), :])
        return jnp.concatenate(parts, axis=0).astype(BF16)

    def process(k_ref, v_ref, mask):
        zs = []
        for g in range(ng):
            qg = q_ref[g * SUBLANES:(g + 1) * SUBLANES, :].astype(BF16)
            s = lax.dot_general(qg, head_rows(k_ref, g), (((1,), (1,)), ((), ())),
                                preferred_element_type=F32)
            z = s[:, :page]
            for p in range(1, hpg):
                z = jnp.where(sub_head == p, s[:, p * page:(p + 1) * page], z)
            zs.append(z)
        z = jnp.concatenate(zs, axis=0) + bias
        sp = _softplus(z)
        if mask is not None:
            sp = jnp.where(mask, sp, 0.0)
        t2 = _suffix_sums(sp, uo)
        r = r_ref[...]
        w = jnp.exp(z - sp - t2[:, :page] - r)
        if mask is not None:
            w = jnp.where(mask, w, 0.0)
        for g in range(ng):
            wg = w[g * SUBLANES:(g + 1) * SUBLANES, :]
            wexp = jnp.concatenate([jnp.where(sub_head == p, wg, 0.0) for p in range(hpg)],
                                   axis=1).astype(BF16)
            acc_ref[g * SUBLANES:(g + 1) * SUBLANES, :] += _dot(wexp, head_rows(v_ref, g))
        r_ref[...] = r + t2[:, page:]

    @pl.when(j == 0)
    def _():
        acc_ref[...] = jnp.zeros_like(acc_ref)
        r_ref[...] = jnp.zeros_like(r_ref)
        kpad_ref[...] = jnp.zeros_like(kpad_ref)
        vpad_ref[...] = jnp.zeros_like(vpad_ref)
        kpad_ref[:, 0:nt * HEAD_BLOCK, :] = knew_ref[...]
        vpad_ref[:, 0:nt * HEAD_BLOCK, :] = vnew_ref[...]
        tok = lax.broadcasted_iota(jnp.int32, (rows, page), 0) % nt
        key = lax.broadcasted_iota(jnp.int32, (rows, page), 1)
        process([kpad_ref.at[bk] for bk in range(nblk)],
                [vpad_ref.at[bk] for bk in range(nblk)], key < tok)

    @pl.when(j > 0)
    def _():
        for g in range(gp):
            process(k_pages[g], v_pages[g], None)

    @pl.when(j == pl.num_programs(1) - 1)
    def _():
        o_ref[...] = acc_ref[...]


def attention_sample(q, k_new, v_new, cache_k, cache_v, page_table, bias, pages_per_step=4):
    b, nt, d = q.shape
    nh = d // HEAD_DIM
    n_pool, page = cache_k.shape[:2]
    n_pages = page_table.shape[1]
    rows = nh * nt
    gp = math.gcd(n_pages, pages_per_step)
    assert SUBLANES % nt == 0 and nh % HEAD_BLOCK == 0
    nblk = nh // HEAD_BLOCK
    q_rows = q.astype(F32).reshape(b, nt, nh, HEAD_DIM).transpose(0, 2, 1, 3).reshape(b, rows, HEAD_DIM)
    bias_rows = jnp.broadcast_to(jnp.repeat(bias, nt)[:, None], (rows, LANES)).astype(F32)
    cache_k = cache_k.reshape(n_pool, page, nh, HEAD_DIM)
    cache_v = cache_v.reshape(n_pool, page, nh, HEAD_DIM)
    blocked = lambda x: x.reshape(b, nt, nblk, HEAD_BLOCK, HEAD_DIM).transpose(0, 2, 1, 3, 4).reshape(
        b, nblk, nt * HEAD_BLOCK, HEAD_DIM)

    def page_spec(g, bk):
        def index_map(i, j, pt):
            return (pt[i, n_pages - 1 - (jnp.maximum(j, 1) - 1) * gp - g], 0, bk, 0)
        return pl.BlockSpec((None, page, HEAD_BLOCK, HEAD_DIM), index_map)

    page_specs = [page_spec(g, bk) for g in range(gp) for bk in range(nblk)]
    per_seq = lambda *shape: pl.BlockSpec((None, *shape), lambda i, j, pt: (i,) + (0,) * len(shape))
    pad_scratch = pltpu.VMEM((nblk, page * HEAD_BLOCK, HEAD_DIM), F32)
    grid_spec = pltpu.PrefetchScalarGridSpec(
        num_scalar_prefetch=1,
        grid=(b, n_pages // gp + 1),
        in_specs=[per_seq(rows, HEAD_DIM),
                  pl.BlockSpec((rows, LANES), lambda i, j, pt: (0, 0)),
                  per_seq(nblk, nt * HEAD_BLOCK, HEAD_DIM), per_seq(nblk, nt * HEAD_BLOCK, HEAD_DIM),
                  *page_specs, *page_specs,
                  pl.BlockSpec((2 * page, page + LANES), lambda i, j, pt: (0, 0))],
        out_specs=per_seq(rows, HEAD_DIM),
        scratch_shapes=[pltpu.VMEM((rows, HEAD_DIM), F32), pltpu.VMEM((rows, LANES), F32),
                        pad_scratch, pad_scratch],
    )
    out = pl.pallas_call(
        functools.partial(_dec_attn_kernel, nt=nt, nh=nh, page=page, gp=gp),
        grid_spec=grid_spec,
        out_shape=jax.ShapeDtypeStruct((b, rows, HEAD_DIM), F32),
        compiler_params=_params("parallel", "arbitrary"),
        name="sb_attention_sample",
    )(page_table, q_rows, bias_rows, blocked(k_new), blocked(v_new),
      *([cache_k] * (gp * nblk)), *([cache_v] * (gp * nblk)), _tail_matrix(page, row_sums=True))
    return out.reshape(b, nh, nt, HEAD_DIM).transpose(0, 2, 1, 3).reshape(b, nt, d)


def _s5_discretize_kernel(are_ref, aim_ref, ldt_ref, bre_ref, bim_ref,
                          lbre_ref, lbim_ref, bbre_ref, bbim_ref):
    a_re = are_ref[...]
    a_im = aim_ref[...]
    dt = jnp.exp(ldt_ref[...])
    mag = jnp.exp(a_re * dt)
    ang = a_im * dt
    lb_re = mag * jnp.cos(ang)
    lb_im = mag * jnp.sin(ang)
    den = a_re * a_re + a_im * a_im
    num_re = lb_re - 1.0
    f_re = (num_re * a_re + lb_im * a_im) / den
    f_im = (lb_im * a_re - num_re * a_im) / den
    lbre_ref[...] = lb_re
    lbim_ref[...] = lb_im
    b_re = bre_ref[...]
    b_im = bim_ref[...]
    bbre_ref[...] = f_re * b_re - f_im * b_im
    bbim_ref[...] = f_re * b_im + f_im * b_re


def s5_discretize(a_re, a_im, log_dt, b_re, b_im):
    g, n = a_re.shape
    p = b_re.shape[-1]
    rep = lambda x: jnp.repeat(x, p, axis=0)
    rows = lambda x: x.transpose(0, 2, 1).reshape(g * p, n)
    sd = jax.ShapeDtypeStruct((g * p, n), F32)
    lb_re, lb_im, bb_re, bb_im = pl.pallas_call(
        _s5_discretize_kernel,
        out_shape=[sd, sd, sd, sd],
        name="s5_discretize",
    )(rep(a_re), rep(a_im), rep(jnp.broadcast_to(log_dt[:, None], (g, n))), rows(b_re), rows(b_im))
    return lb_re[::p], lb_im[::p], bb_re.reshape(g, p, n), bb_im.reshape(g, p, n)


def _s5_block_operands(bb_re, bb_im, c_re, c_im):
    g, p, n = bb_re.shape
    nq = g // GROUPS_PER_QUAD
    eye = jnp.eye(GROUPS_PER_QUAD, dtype=F32)

    def b_side(x):
        x = x.reshape(nq, GROUPS_PER_QUAD, p, n)
        return (x[:, :, :, None, :] * eye[None, :, None, :, None]).reshape(
            nq, GROUPS_PER_QUAD * p, GROUPS_PER_QUAD * n)

    def c_side(x):
        x = x.reshape(nq, GROUPS_PER_QUAD, p, n).transpose(0, 1, 3, 2)
        return (x[:, :, :, None, :] * eye[None, :, None, :, None]).reshape(
            nq, GROUPS_PER_QUAD * n, GROUPS_PER_QUAD * p)

    bq = jnp.concatenate([b_side(bb_re), b_side(bb_im)], axis=2)
    cq = jnp.concatenate([c_side(c_re), -c_side(c_im)], axis=1)
    return bq, cq


def _s5_input_phase(u_ref, bq_ref, bre_scr, bim_scr, rows, pitch, precise):
    nq = bq_ref.shape[0]
    half = PAIRS_PER_QUAD * LANES
    for qd in range(nq):
        uq = u_ref[:, qd * LANES:(qd + 1) * LANES]
        if precise:
            bu = _dot_x3(uq, bq_ref[qd])
        else:
            bu = _dot(uq.astype(BF16), bq_ref[qd])
        for jj in range(PAIRS_PER_QUAD):
            r0 = (PAIRS_PER_QUAD * qd + jj) * pitch
            bre_scr[r0:r0 + rows, :] = bu[:, jj * LANES:(jj + 1) * LANES]
            bim_scr[r0:r0 + rows, :] = bu[:, half + jj * LANES:half + (jj + 1) * LANES]


def _s5_output_phase(u_ref, cq_ref, dskip_ref, y_ref, hre_scr, him_scr, rows, pitch, precise):
    nq = cq_ref.shape[0]
    for qd in range(nq):
        parts = []
        for scr in (hre_scr, him_scr):
            for jj in range(PAIRS_PER_QUAD):
                r0 = (PAIRS_PER_QUAD * qd + jj) * pitch
                parts.append(scr[r0:r0 + rows, :])
        hcat = jnp.concatenate(parts, axis=1)
        if precise:
            yq = _dot_x3(hcat, cq_ref[qd])
        else:
            yq = _dot(hcat.astype(BF16), cq_ref[qd])
        sl = slice(qd * LANES, (qd + 1) * LANES)
        y_ref[:, sl] = yq + dskip_ref[:, sl] * u_ref[:, sl]


def _s5_prompt_kernel(u_ref, bq_ref, cq_ref, lbre_ref, lbim_ref, dskip_ref, h0re_ref, h0im_ref,
                      y_ref, hre_out, him_out, bre_scr, bim_scr, stre_scr, stim_scr, *, tc):
    i = pl.program_id(0)
    n_pairs = lbre_ref.shape[0]

    @pl.when(i == 0)
    def _():
        stre_scr[...] = h0re_ref[...]
        stim_scr[...] = h0im_ref[...]

    pitch = tc + STRIDE_PAD
    _s5_input_phase(u_ref, bq_ref, bre_scr, bim_scr, tc, pitch, False)

    lr = lbre_ref[...]
    li = lbim_ref[...]

    def step(t, carry):
        hr, hi = carry
        idx = pl.ds(t, n_pairs, stride=pitch)
        nr = lr * hr - li * hi + bre_scr[idx, :]
        ni = lr * hi + li * hr + bim_scr[idx, :]
        bre_scr[idx, :] = nr
        bim_scr[idx, :] = ni
        return nr, ni

    hr, hi = lax.fori_loop(0, tc, step, (stre_scr[...], stim_scr[...]))
    stre_scr[...] = hr
    stim_scr[...] = hi
    hre_out[...] = hr
    him_out[...] = hi

    _s5_output_phase(u_ref, cq_ref, dskip_ref, y_ref, bre_scr, bim_scr, tc, pitch, False)


def s5_prompt(u, col0, bq, cq, lb_re, lb_im, d_skip, h0_re, h0_im, tc=256):
    t = u.shape[0]
    n_pairs = lb_re.shape[0]
    d = n_pairs * 2 * GROUP_CH
    tc = min(tc, t)
    nq = bq.shape[0]
    whole = lambda *shape: pl.BlockSpec(shape, lambda i: (0,) * len(shape))
    sd = jax.ShapeDtypeStruct
    return pl.pallas_call(
        functools.partial(_s5_prompt_kernel, tc=tc),
        grid=(t // tc,),
        in_specs=[pl.BlockSpec((tc, d), lambda i: (i, col0 // d)),
                  whole(*bq.shape), whole(*cq.shape),
                  whole(n_pairs, LANES), whole(n_pairs, LANES), whole(1, d),
                  whole(n_pairs, LANES), whole(n_pairs, LANES)],
        out_specs=[pl.BlockSpec((tc, d), lambda i: (i, 0)),
                   whole(n_pairs, LANES), whole(n_pairs, LANES)],
        out_shape=[sd((t, d), F32), sd((n_pairs, LANES), F32), sd((n_pairs, LANES), F32)],
        scratch_shapes=[pltpu.VMEM((n_pairs * (tc + STRIDE_PAD), LANES), F32),
                        pltpu.VMEM((n_pairs * (tc + STRIDE_PAD), LANES), F32),
                        pltpu.VMEM((n_pairs, LANES), F32), pltpu.VMEM((n_pairs, LANES), F32)],
        compiler_params=_params("arbitrary"),
        name="s5_prompt",
    )(u, bq.astype(BF16), cq.astype(BF16), lb_re, lb_im, d_skip.reshape(1, d), h0_re, h0_im)


def _s5_sample_kernel(u_ref, bq_ref, cq_ref, lbre_ref, lbim_ref, dskip_ref, h0re_ref, h0im_ref,
                      y_ref, hre_out, him_out, bre_scr, bim_scr, *, nt, nb):
    rows = nt * nb
    n_pairs = lbre_ref.shape[0]
    _s5_input_phase(u_ref, bq_ref, bre_scr, bim_scr, rows, rows, True)

    def pair_body(pr, carry):
        lr = lbre_ref[pl.ds(pr, 1), :]
        li = lbim_ref[pl.ds(pr, 1), :]
        hr = h0re_ref[pr]
        hi = h0im_ref[pr]
        for t in range(nt):
            idx = pl.ds(pl.multiple_of(pr * rows + t * nb, 8), nb)
            nr = lr * hr - li * hi + bre_scr[idx, :]
            ni = lr * hi + li * hr + bim_scr[idx, :]
            bre_scr[idx, :] = nr
            bim_scr[idx, :] = ni
            hr, hi = nr, ni
        hre_out[pr] = hr
        him_out[pr] = hi
        return carry

    lax.fori_loop(0, n_pairs, pair_body, 0)
    _s5_output_phase(u_ref, cq_ref, dskip_ref, y_ref, bre_scr, bim_scr, rows, rows, False)


def s5_sample(u, bq, cq, lb_re, lb_im, d_skip, h0_re, h0_im, nt, nb):
    rows, d = u.shape
    n_pairs = lb_re.shape[0]
    sd = jax.ShapeDtypeStruct
    return pl.pallas_call(
        functools.partial(_s5_sample_kernel, nt=nt, nb=nb),
        out_shape=[sd((rows, d), F32), sd((n_pairs, nb, LANES), F32), sd((n_pairs, nb, LANES), F32)],
        scratch_shapes=[pltpu.VMEM((n_pairs * rows, LANES), F32), pltpu.VMEM((n_pairs * rows, LANES), F32)],
        compiler_params=pltpu.CompilerParams(vmem_limit_bytes=VMEM_LIMIT),
        name="s5_sample",
    )(u, bq, cq.astype(BF16), lb_re, lb_im, d_skip.reshape(1, d), h0_re, h0_im)


def _rms(x, g):
    ms = jnp.mean(x * x, axis=-1, keepdims=True)
    return x * lax.rsqrt(ms + EPS) * g


def _mix_kernel(att_ref, y_ref, ga_ref, gs_ref, wglu_ref, bglu_ref, gatt_ref, gssm_ref,
                attn_ref, ssmn_ref):
    attn_ref[...] = (_rms(att_ref[...], gatt_ref[...]) * jax.nn.silu(ga_ref[...])).astype(attn_ref.dtype)
    y = jax.nn.gelu(y_ref[...])
    y = y * jax.nn.sigmoid(_dot(y.astype(BF16), wglu_ref[...]) + bglu_ref[...])
    ssmn_ref[...] = (_rms(y, gssm_ref[...]) * jax.nn.silu(gs_ref[...])).astype(ssmn_ref.dtype)


def mix_post(att, y, gug, w_glu, b_glu, g_att, g_ssm, tm=256):
    m, d = att.shape
    tm = min(tm, m)
    row = lambda c: pl.BlockSpec((tm, d), lambda i: (i, c))
    vec = pl.BlockSpec((1, d), lambda i: (0, 0))
    return pl.pallas_call(
        _mix_kernel,
        grid=(m // tm,),
        in_specs=[row(0), row(0), row(0), row(2),
                  pl.BlockSpec((d, d), lambda i: (0, 0)), vec, vec, vec],
        out_specs=[row(0), row(0)],
        out_shape=[jax.ShapeDtypeStruct((m, d), BF16), jax.ShapeDtypeStruct((m, d), BF16)],
        compiler_params=_params("parallel"),
        name="mix_post",
    )(att, y, gug, gug, w_glu, b_glu.reshape(1, d), g_att.reshape(1, d), g_ssm.reshape(1, d))


def _out_kernel(x_ref, a_ref, s_ref, wa_ref, ws_ref, g_ref, o_ref, xn_ref, rows_scr, ssq_scr, *, tn):
    j = pl.program_id(1)
    h = x_ref[...] + (_dot(a_ref[...], wa_ref[...]) + _dot(s_ref[...], ws_ref[...]))
    o_ref[...] = h
    _row_norm_collect(j, h, rows_scr, ssq_scr)

    @pl.when(j == pl.num_programs(1) - 1)
    def _():
        _row_norm_finish(rows_scr, ssq_scr, g_ref, xn_ref, tn)


def out_project(x, att_n, ssm_n, w_out, g_next, tm=512, tn=1024):
    m, dm = x.shape
    dh = att_n.shape[1]
    tm = min(tm, m)
    tn = min(tn, dm)
    return pl.pallas_call(
        functools.partial(_out_kernel, tn=tn),
        grid=(m // tm, dm // tn),
        in_specs=[pl.BlockSpec((tm, tn), lambda i, j: (i, j)),
                  pl.BlockSpec((tm, dh), lambda i, j: (i, 0)),
                  pl.BlockSpec((tm, dh), lambda i, j: (i, 0)),
                  pl.BlockSpec((dh, tn), lambda i, j: (0, j)),
                  pl.BlockSpec((dh, tn), lambda i, j: (1, j)),
                  pl.BlockSpec((1, dm), lambda i, j: (0, 0))],
        out_specs=[pl.BlockSpec((tm, tn), lambda i, j: (i, j)),
                   pl.BlockSpec((tm, dm), lambda i, j: (i, 0))],
        out_shape=[jax.ShapeDtypeStruct((m, dm), F32), jax.ShapeDtypeStruct((m, dm), BF16)],
        scratch_shapes=[pltpu.VMEM((dm // tn, tm, tn), F32), pltpu.VMEM((tm, 1), F32)],
        compiler_params=_params("parallel", "arbitrary"),
        name="out_proj",
    )(x, att_n, ssm_n, w_out, w_out, g_next.reshape(1, dm))


def _row_norm_finish(rows_scr, ssq_scr, g_ref, o_ref, tn):
    d = o_ref.shape[1]
    inv = lax.rsqrt(ssq_scr[...] * (1.0 / d) + EPS)
    for jj in range(d // tn):
        sl = slice(jj * tn, (jj + 1) * tn)
        o_ref[:, sl] = (rows_scr[jj] * inv * g_ref[:, sl]).astype(o_ref.dtype)


def _row_norm_collect(j, val, rows_scr, ssq_scr):
    part = jnp.sum(val * val, axis=-1, keepdims=True)

    @pl.when(j == 0)
    def _():
        ssq_scr[...] = part

    @pl.when(j > 0)
    def _():
        ssq_scr[...] += part

    rows_scr[j] = val


def _ple_kernel(h_ref, xn_ref, p_ref, wg_ref, wp_ref, gf_ref, o_ref, rows_scr, ssq_scr, *, tn):
    j = pl.program_id(1)
    gate = jax.nn.sigmoid(_dot(xn_ref[...], wg_ref[...]))
    ple = _dot(p_ref[...].astype(BF16), wp_ref[...])
    _row_norm_collect(j, h_ref[...] + ple * gate, rows_scr, ssq_scr)

    @pl.when(j == pl.num_programs(1) - 1)
    def _():
        _row_norm_finish(rows_scr, ssq_scr, gf_ref, o_ref, tn)


def ple_gate_final(h, xn, p, w_gate, w_ple, g_final, tm=512, tn=512):
    m, dm = h.shape
    dp = p.shape[1]
    tm = min(tm, m)
    tn = min(tn, dm)
    return pl.pallas_call(
        functools.partial(_ple_kernel, tn=tn),
        grid=(m // tm, dm // tn),
        in_specs=[pl.BlockSpec((tm, tn), lambda i, j: (i, j)),
                  pl.BlockSpec((tm, dm), lambda i, j: (i, 0)),
                  pl.BlockSpec((tm, dp), lambda i, j: (i, 0)),
                  pl.BlockSpec((dm, tn), lambda i, j: (0, j)),
                  pl.BlockSpec((dp, tn), lambda i, j: (0, j)),
                  pl.BlockSpec((1, dm), lambda i, j: (0, 0))],
        out_specs=pl.BlockSpec((tm, dm), lambda i, j: (i, 0)),
        out_shape=jax.ShapeDtypeStruct((m, dm), F32),
        scratch_shapes=[pltpu.VMEM((dm // tn, tm, tn), F32), pltpu.VMEM((tm, 1), F32)],
        compiler_params=_params("parallel", "arbitrary"),
        name="ple_gate_final",
    )(h, xn, p, w_gate, w_ple, g_final.reshape(1, dm))


def _pair_layout(x):
    return x.reshape(*x.shape[:-2], x.shape[-2] // 2, 2 * x.shape[-1])


def kernel(x_prompt, x_sample, cache_k, cache_v, state_ssm_re, state_ssm_im, page_table, p_prompt, p_sample, g_in, w_in, att_bias, a_re, a_im, log_dt, b_re, b_im, c_re, c_im, d_skip, w_glu, b_glu, g_att, g_ssm, w_out, g_ple, w_ple, w_ple_gate, g_final):
    depth = w_in.shape[0]
    assert depth == 1, "single-layer step"
    nb_p, t_p, dm = x_prompt.shape
    assert nb_p == 1, "one prompt sequence"
    nb, nt, _ = x_sample.shape
    n_groups, n_state = a_re.shape[1:]
    d_ssm = n_groups * GROUP_CH
    d_att = dm - d_ssm
    nh = d_att // HEAD_DIM
    page = cache_k.shape[2]
    scale = 1.0 / math.sqrt(HEAD_DIM)

    w_in_b = w_in[0].astype(BF16)
    w_glu_b = w_glu[0].astype(BF16)
    w_out_b = w_out[0].astype(BF16)
    w_ple_b = w_ple[0].astype(BF16)
    w_gate_b = w_ple_gate[0].astype(BF16)

    lb_re, lb_im, bb_re, bb_im = s5_discretize(a_re[0], a_im[0], log_dt[0], b_re[0], b_im[0])
    bq, cq = _s5_block_operands(bb_re, bb_im, c_re[0], c_im[0])
    lb_re_p, lb_im_p = _pair_layout(lb_re), _pair_layout(lb_im)
    dsk = d_skip[0].reshape(-1)

    def in_proj(x2d):
        xn = rmsnorm_rows(x2d, g_in[0], BF16)
        (q,) = project(xn, w_in_b, 0, d_att, [BF16], scale=scale)
        k, k_b = project(xn, w_in_b, d_att, d_att, [F32, BF16])
        v, v_b = project(xn, w_in_b, 2 * d_att, d_att, [F32, BF16])
        (gug,) = project(xn, w_in_b, 3 * d_att, d_att + 2 * d_ssm, [F32])
        return q, k, k_b, v, v_b, gug

    def tail(x2d, p2d, att, y, gug):
        att_n, ssm_n = mix_post(att, y, gug, w_glu_b, b_glu[0], g_att[0], g_ssm[0])
        h2, xn2 = out_project(x2d, att_n, ssm_n, w_out_b, g_ple[0])
        return ple_gate_final(h2, xn2, p2d, w_gate_b, w_ple_b, g_final)

    xp = x_prompt.reshape(t_p, dm)
    q, k, k_b, v, v_b, gug = in_proj(xp)
    att = attention_prompt(q, k_b, v_b, att_bias[0])
    zero_state = jnp.zeros((n_groups // 2, 2 * n_state), F32)
    y, hre_p, him_p = s5_prompt(gug, d_att, bq, cq, lb_re_p, lb_im_p, dsk, zero_state, zero_state)
    y_prompt = tail(xp, p_prompt[0, 0], att, y, gug).reshape(1, t_p, dm)
    new_k_prompt = k.reshape(1, 1, t_p, nh, HEAD_DIM)
    new_v_prompt = v.reshape(1, 1, t_p, nh, HEAD_DIM)
    new_re_prompt = hre_p.reshape(1, 1, n_groups, n_state)
    new_im_prompt = him_p.reshape(1, 1, n_groups, n_state)

    xs = x_sample.reshape(nb * nt, dm)
    q, k, _, v, _, gug = in_proj(xs)
    att = attention_sample(
        q.reshape(nb, nt, d_att), k.reshape(nb, nt, d_att), v.reshape(nb, nt, d_att),
        cache_k[0], cache_v[0], page_table, att_bias[0]).reshape(nb * nt, d_att)
    u_tb = gug[:, d_att:d_att + d_ssm].reshape(nb, nt, d_ssm).transpose(1, 0, 2).reshape(nt * nb, d_ssm)
    h0_re = _pair_layout(state_ssm_re[0]).transpose(1, 0, 2)
    h0_im = _pair_layout(state_ssm_im[0]).transpose(1, 0, 2)
    y_tb, hre_s, him_s = s5_sample(u_tb, bq, cq, lb_re_p, lb_im_p, dsk, h0_re, h0_im, nt, nb)
    y = y_tb.reshape(nt, nb, d_ssm).transpose(1, 0, 2).reshape(nb * nt, d_ssm)
    y_sample = tail(xs, p_sample[0].reshape(nb * nt, -1), att, y, gug).reshape(nb, nt, dm)
    new_k_sample = k.reshape(1, nb, nt, nh, HEAD_DIM)
    new_v_sample = v.reshape(1, nb, nt, nh, HEAD_DIM)
    new_re_sample = hre_s.transpose(1, 0, 2).reshape(1, nb, n_groups, n_state)
    new_im_sample = him_s.transpose(1, 0, 2).reshape(1, nb, n_groups, n_state)

    return (y_prompt, y_sample, new_k_prompt, new_v_prompt, new_k_sample, new_v_sample,
            new_re_prompt, new_im_prompt, new_re_sample, new_im_sample)
```

```python
import functools
import math

import jax
import jax.numpy as jnp
from jax import lax
from jax.experimental import pallas as pl
from jax.experimental.pallas import tpu as pltpu

F32 = jnp.float32
BF16 = jnp.bfloat16

HEAD_DIM = 128
GROUP_CH = 16
STATE_N = 64
EPS = 1e-6
LANES = 128
VMEM_LIMIT = 56 * 1024 * 1024
GROUPS_PER_QUAD = LANES // GROUP_CH
PAIRS_PER_QUAD = GROUPS_PER_QUAD // 2
SUBLANES = 8
HEAD_BLOCK = SUBLANES
STRIDE_PAD = 4


def _params(*sem):
    return pltpu.CompilerParams(dimension_semantics=sem, vmem_limit_bytes=VMEM_LIMIT)


def _dot(a, b):
    return jnp.dot(a, b, preferred_element_type=F32)


def _split_bf16(x):
    hi = x.astype(BF16)
    lo = (x - hi.astype(F32)).astype(BF16)
    return hi, lo


def _dot_x3(a, b):
    ah, al = _split_bf16(a)
    bh, bl = _split_bf16(b)
    return _dot(ah, bh) + _dot(al, bh) + _dot(ah, bl)


def _softplus(z):
    return jnp.maximum(z, 0.0) + jnp.log(1.0 + jnp.exp2(jnp.abs(z) * -math.log2(math.e)))


def _rms_kernel(x_ref, g_ref, o_ref):
    x = x_ref[...]
    ms = jnp.mean(x * x, axis=-1, keepdims=True)
    o_ref[...] = (x * lax.rsqrt(ms + EPS) * g_ref[...]).astype(o_ref.dtype)


def rmsnorm_rows(x, g, out_dtype, tm=256):
    m, d = x.shape
    tm = min(tm, m)
    return pl.pallas_call(
        _rms_kernel,
        grid=(m // tm,),
        in_specs=[pl.BlockSpec((tm, d), lambda i: (i, 0)),
                  pl.BlockSpec((1, d), lambda i: (0, 0))],
        out_specs=pl.BlockSpec((tm, d), lambda i: (i, 0)),
        out_shape=jax.ShapeDtypeStruct((m, d), out_dtype),
        compiler_params=_params("parallel"),
        name="rmsnorm",
    )(x, g.reshape(1, d))


def _proj_kernel(x_ref, w_ref, *o_refs, scale):
    acc = _dot(x_ref[...], w_ref[...])
    if scale != 1.0:
        acc = acc * scale
    for o_ref in o_refs:
        o_ref[...] = acc.astype(o_ref.dtype)


def project(xn, w, col0, n, out_dtypes, scale=1.0, tm=1024, tn=1024):
    m, k = xn.shape
    tm = min(tm, m)
    tn = math.gcd(tn, n, col0)
    off = col0 // tn
    return pl.pallas_call(
        functools.partial(_proj_kernel, scale=scale),
        grid=(m // tm, n // tn),
        in_specs=[pl.BlockSpec((tm, k), lambda i, j: (i, 0)),
                  pl.BlockSpec((k, tn), lambda i, j: (0, off + j))],
        out_specs=[pl.BlockSpec((tm, tn), lambda i, j: (i, j)) for _ in out_dtypes],
        out_shape=[jax.ShapeDtypeStruct((m, n), dt) for dt in out_dtypes],
        compiler_params=_params("parallel", "arbitrary"),
        name="in_proj",
    )(xn, w)


def _tail_matrix(tk, row_sums, split):
    j = jnp.arange(tk)[:, None]
    s = jnp.arange(tk)[None, :]
    u = (j > s).astype(BF16)
    if row_sums:
        u = jnp.concatenate([u, jnp.ones((tk, LANES), BF16)], axis=1)
    return jnp.concatenate([u, u], axis=0) if split else u


def _suffix_sums(sp, uo):
    if uo.shape[0] == 2 * sp.shape[1]:
        return _dot(jnp.concatenate(_split_bf16(sp), axis=1), uo)
    return _dot(sp.astype(BF16), uo)


def _sb_tile(q, k, v, bias, uo, r, mask, tk):
    s = lax.dot_general(q, k, (((1,), (1,)), ((), ())), preferred_element_type=F32)
    z = s + bias
    sp = _softplus(z)
    if mask is not None:
        sp = jnp.where(mask, sp, 0.0)
    t2 = _suffix_sums(sp, uo)
    rb = jnp.concatenate([r] * (tk // LANES), axis=1) if tk > LANES else r
    w = jnp.exp(z - sp - t2[:, :tk] - rb)
    if mask is not None:
        w = jnp.where(mask, w, 0.0)
    if uo.shape[1] > tk:
        rs = t2[:, tk:]
    else:
        rs = jnp.broadcast_to(jnp.sum(sp, axis=1, keepdims=True), r.shape)
    return _dot(w.astype(BF16), v), rs


def _attn_kernel(bias_ref, q_ref, k_ref, v_ref, uo_ref, o_ref, acc_ref, r_ref, *, tq, tk):
    h = pl.program_id(0)
    qi = pl.program_id(1)
    bias = bias_ref[h]
    q = q_ref[...]
    uo = uo_ref[...]
    acc_ref[...] = jnp.zeros_like(acc_ref)
    r_ref[...] = jnp.zeros_like(r_ref)
    nd = tq // tk

    def tile(kt, row0, mask):
        start = pl.multiple_of(kt * tk, tk)
        k = k_ref[pl.ds(start, tk), :]
        v = v_ref[pl.ds(start, tk), :]
        r = r_ref[row0:, :]
        pv, rs = _sb_tile(q[row0:, :], k, v, bias, uo, r, mask, tk)
        acc_ref[row0:, :] += pv
        r_ref[row0:, :] = r + rs

    for d in range(nd - 1, -1, -1):
        row0 = d * tk
        row = lax.broadcasted_iota(jnp.int32, (tq - row0, tk), 0)
        col = lax.broadcasted_iota(jnp.int32, (tq - row0, tk), 1)
        tile(qi * nd + d, row0, col < row)

    def body(i, carry):
        for d in range(nd):
            tile((qi - i) * nd - 1 - d, 0, None)
        return carry

    lax.fori_loop(0, qi, body, 0)
    o_ref[...] = acc_ref[...]


def attention_prompt(q, k, v, bias, tq=2048, tk=256):
    t, d = q.shape
    nh = d // HEAD_DIM
    tq = min(tq, t)
    tk = min(tk, tq)
    return pl.pallas_call(
        functools.partial(_attn_kernel, tq=tq, tk=tk),
        grid=(nh, t // tq),
        in_specs=[pl.BlockSpec(memory_space=pltpu.SMEM),
                  pl.BlockSpec((tq, HEAD_DIM), lambda h, i: (i, h)),
                  pl.BlockSpec((t, HEAD_DIM), lambda h, i: (0, h)),
                  pl.BlockSpec((t, HEAD_DIM), lambda h, i: (0, h)),
                  pl.BlockSpec((tk, tk), lambda h, i: (0, 0))],
        out_specs=pl.BlockSpec((tq, HEAD_DIM), lambda h, i: (i, h)),
        out_shape=jax.ShapeDtypeStruct((t, d), F32),
        scratch_shapes=[pltpu.VMEM((tq, HEAD_DIM), F32), pltpu.VMEM((tq, LANES), F32)],
        compiler_params=_params("parallel", "arbitrary"),
        name="sb_attention_prompt",
    )(bias, q, k, v, _tail_matrix(tk, row_sums=False, split=False))


def _dec_attn_kernel(pt_ref, q_ref, bias_ref, knew_ref, vnew_ref, *rest, nt, nh, page, gp):
    del pt_ref
    nblk = nh // HEAD_BLOCK
    npb = gp * nblk
    flat = lambda ref: ref.reshape(page * HEAD_BLOCK, HEAD_DIM)
    k_pages = [[flat(r) for r in rest[g * nblk:(g + 1) * nblk]] for g in range(gp)]
    v_pages = [[flat(r) for r in rest[npb + g * nblk:npb + (g + 1) * nblk]] for g in range(gp)]
    uo_ref, o_ref, acc_ref, r_ref, kpad_ref, vpad_ref = rest[2 * npb:]
    j = pl.program_id(1)
    rows = nh * nt
    hpg = SUBLANES // nt
    ng = rows // SUBLANES
    bias = bias_ref[...]
    uo = uo_ref[...]
    sub_head = lax.broadcasted_iota(jnp.int32, (SUBLANES, page), 0) // nt

    def head_rows(blocks, g):
        parts = []
        for p in range(hpg):
            blk, hb = divmod(g * hpg + p, HEAD_BLOCK)
            parts.append(blocks[blk][pl.ds(hb, page, stride=HEAD_BLOCK), :])
        return jnp.concatenate(parts, axis=0).astype(BF16)

    def process(k_ref, v_ref, mask):
        zs = []
        for g in range(ng):
            qg = q_ref[g * SUBLANES:(g + 1) * SUBLANES, :].astype(BF16)
            s = lax.dot_general(qg, head_rows(k_ref, g), (((1,), (1,)), ((), ())),
                                preferred_element_type=F32)
            z = s[:, :page]
            for p in range(1, hpg):
                z = jnp.where(sub_head == p, s[:, p * page:(p + 1) * page], z)
            zs.append(z)
        z = jnp.concatenate(zs, axis=0) + bias
        sp = _softplus(z)
        if mask is not None:
            sp = jnp.where(mask, sp, 0.0)
        t2 = _suffix_sums(sp, uo)
        r = r_ref[...]
        w = jnp.exp(z - sp - t2[:, :page] - r)
        if mask is not None:
            w = jnp.where(mask, w, 0.0)
        for g in range(ng):
            wg = w[g * SUBLANES:(g + 1) * SUBLANES, :]
            wexp = jnp.concatenate([jnp.where(sub_head == p, wg, 0.0) for p in range(hpg)],
                                   axis=1).astype(BF16)
            acc_ref[g * SUBLANES:(g + 1) * SUBLANES, :] += _dot(wexp, head_rows(v_ref, g))
        r_ref[...] = r + t2[:, page:]

    @pl.when(j == 0)
    def _():
        acc_ref[...] = jnp.zeros_like(acc_ref)
        r_ref[...] = jnp.zeros_like(r_ref)
        kpad_ref[...] = jnp.zeros_like(kpad_ref)
        vpad_ref[...] = jnp.zeros_like(vpad_ref)
        kpad_ref[:, 0:nt * HEAD_BLOCK, :] = knew_ref[...]
        vpad_ref[:, 0:nt * HEAD_BLOCK, :] = vnew_ref[...]
        tok = lax.broadcasted_iota(jnp.int32, (rows, page), 0) % nt
        key = lax.broadcasted_iota(jnp.int32, (rows, page), 1)
        process([kpad_ref.at[bk] for bk in range(nblk)],
                [vpad_ref.at[bk] for bk in range(nblk)], key < tok)

    @pl.when(j > 0)
    def _():
        for g in range(gp):
            process(k_pages[g], v_pages[g], None)

    @pl.when(j == pl.num_programs(1) - 1)
    def _():
        o_ref[...] = acc_ref[...]


def attention_sample(q, k_new, v_new, cache_k, cache_v, page_table, bias, pages_per_step=8):
    b, nt, d = q.shape
    nh = d // HEAD_DIM
    n_pool, page = cache_k.shape[:2]
    n_pages = page_table.shape[1]
    rows = nh * nt
    gp = math.gcd(n_pages, pages_per_step)
    assert SUBLANES % nt == 0 and nh % HEAD_BLOCK == 0
    nblk = nh // HEAD_BLOCK
    q_rows = q.astype(F32).reshape(b, nt, nh, HEAD_DIM).transpose(0, 2, 1, 3).reshape(b, rows, HEAD_DIM)
    bias_rows = jnp.broadcast_to(jnp.repeat(bias, nt)[:, None], (rows, LANES)).astype(F32)
    cache_k = cache_k.reshape(n_pool, page, nh, HEAD_DIM)
    cache_v = cache_v.reshape(n_pool, page, nh, HEAD_DIM)
    blocked = lambda x: x.reshape(b, nt, nblk, HEAD_BLOCK, HEAD_DIM).transpose(0, 2, 1, 3, 4).reshape(
        b, nblk, nt * HEAD_BLOCK, HEAD_DIM)

    def page_spec(g, bk):
        def index_map(i, j, pt):
            return (pt[i, n_pages - 1 - (jnp.maximum(j, 1) - 1) * gp - g], 0, bk, 0)
        return pl.BlockSpec((None, page, HEAD_BLOCK, HEAD_DIM), index_map)

    page_specs = [page_spec(g, bk) for g in range(gp) for bk in range(nblk)]
    per_seq = lambda *shape: pl.BlockSpec((None, *shape), lambda i, j, pt: (i,) + (0,) * len(shape))
    pad_scratch = pltpu.VMEM((nblk, page * HEAD_BLOCK, HEAD_DIM), F32)
    grid_spec = pltpu.PrefetchScalarGridSpec(
        num_scalar_prefetch=1,
        grid=(b, n_pages // gp + 1),
        in_specs=[per_seq(rows, HEAD_DIM),
                  pl.BlockSpec((rows, LANES), lambda i, j, pt: (0, 0)),
                  per_seq(nblk, nt * HEAD_BLOCK, HEAD_DIM), per_seq(nblk, nt * HEAD_BLOCK, HEAD_DIM),
                  *page_specs, *page_specs,
                  pl.BlockSpec((2 * page, page + LANES), lambda i, j, pt: (0, 0))],
        out_specs=per_seq(rows, HEAD_DIM),
        scratch_shapes=[pltpu.VMEM((rows, HEAD_DIM), F32), pltpu.VMEM((rows, LANES), F32),
                        pad_scratch, pad_scratch],
    )
    out = pl.pallas_call(
        functools.partial(_dec_attn_kernel, nt=nt, nh=nh, page=page, gp=gp),
        grid_spec=grid_spec,
        out_shape=jax.ShapeDtypeStruct((b, rows, HEAD_DIM), F32),
        compiler_params=_params("parallel", "arbitrary"),
        name="sb_attention_sample",
    )(page_table, q_rows, bias_rows, blocked(k_new), blocked(v_new),
      *([cache_k] * (gp * nblk)), *([cache_v] * (gp * nblk)), _tail_matrix(page, row_sums=True, split=True))
    return out.reshape(b, nh, nt, HEAD_DIM).transpose(0, 2, 1, 3).reshape(b, nt, d)


def _s5_discretize_kernel(are_ref, aim_ref, ldt_ref, bre_ref, bim_ref,
                          lbre_ref, lbim_ref, bbre_ref, bbim_ref):
    a_re = are_ref[...]
    a_im = aim_ref[...]
    dt = jnp.exp(ldt_ref[...])
    mag = jnp.exp(a_re * dt)
    ang = a_im * dt
    lb_re = mag * jnp.cos(ang)
    lb_im = mag * jnp.sin(ang)
    den = a_re * a_re + a_im * a_im
    num_re = lb_re - 1.0
    f_re = (num_re * a_re + lb_im * a_im) / den
    f_im = (lb_im * a_re - num_re * a_im) / den
    lbre_ref[...] = lb_re
    lbim_ref[...] = lb_im
    b_re = bre_ref[...]
    b_im = bim_ref[...]
    bbre_ref[...] = f_re * b_re - f_im * b_im
    bbim_ref[...] = f_re * b_im + f_im * b_re


def s5_discretize(a_re, a_im, log_dt, b_re, b_im):
    g, n = a_re.shape
    p = b_re.shape[-1]
    rep = lambda x: jnp.repeat(x, p, axis=0)
    rows = lambda x: x.transpose(0, 2, 1).reshape(g * p, n)
    sd = jax.ShapeDtypeStruct((g * p, n), F32)
    lb_re, lb_im, bb_re, bb_im = pl.pallas_call(
        _s5_discretize_kernel,
        out_shape=[sd, sd, sd, sd],
        name="s5_discretize",
    )(rep(a_re), rep(a_im), rep(jnp.broadcast_to(log_dt[:, None], (g, n))), rows(b_re), rows(b_im))
    return lb_re[::p], lb_im[::p], bb_re.reshape(g, p, n), bb_im.reshape(g, p, n)


def _s5_block_operands(bb_re, bb_im, c_re, c_im):
    g, p, n = bb_re.shape
    nq = g // GROUPS_PER_QUAD
    eye = jnp.eye(GROUPS_PER_QUAD, dtype=F32)

    def b_side(x):
        x = x.reshape(nq, GROUPS_PER_QUAD, p, n)
        return (x[:, :, :, None, :] * eye[None, :, None, :, None]).reshape(
            nq, GROUPS_PER_QUAD * p, GROUPS_PER_QUAD * n)

    def c_side(x):
        x = x.reshape(nq, GROUPS_PER_QUAD, p, n).transpose(0, 1, 3, 2)
        return (x[:, :, :, None, :] * eye[None, :, None, :, None]).reshape(
            nq, GROUPS_PER_QUAD * n, GROUPS_PER_QUAD * p)

    bq = jnp.concatenate([b_side(bb_re), b_side(bb_im)], axis=2)
    cq = jnp.concatenate([c_side(c_re), -c_side(c_im)], axis=1)
    return bq, cq


def _s5_input_phase(u_ref, bq_ref, bre_scr, bim_scr, rows, pitch, precise):
    nq = bq_ref.shape[0]
    half = PAIRS_PER_QUAD * LANES
    for qd in range(nq):
        uq = u_ref[:, qd * LANES:(qd + 1) * LANES]
        if precise:
            bu = _dot_x3(uq, bq_ref[qd])
        else:
            bu = _dot(uq.astype(BF16), bq_ref[qd])
        for jj in range(PAIRS_PER_QUAD):
            r0 = (PAIRS_PER_QUAD * qd + jj) * pitch
            bre_scr[r0:r0 + rows, :] = bu[:, jj * LANES:(jj + 1) * LANES]
            bim_scr[r0:r0 + rows, :] = bu[:, half + jj * LANES:half + (jj + 1) * LANES]


def _s5_output_phase(u_ref, cq_ref, dskip_ref, y_ref, hre_scr, him_scr, rows, pitch, precise):
    nq = cq_ref.shape[0]
    for qd in range(nq):
        parts = []
        for scr in (hre_scr, him_scr):
            for jj in range(PAIRS_PER_QUAD):
                r0 = (PAIRS_PER_QUAD * qd + jj) * pitch
                parts.append(scr[r0:r0 + rows, :])
        hcat = jnp.concatenate(parts, axis=1)
        if precise:
            yq = _dot_x3(hcat, cq_ref[qd])
        else:
            yq = _dot(hcat.astype(BF16), cq_ref[qd])
        sl = slice(qd * LANES, (qd + 1) * LANES)
        y_ref[:, sl] = yq + dskip_ref[:, sl] * u_ref[:, sl]


def _s5_prompt_kernel(u_ref, bq_ref, cq_ref, lbre_ref, lbim_ref, dskip_ref, h0re_ref, h0im_ref,
                      y_ref, hre_out, him_out, bre_scr, bim_scr, stre_scr, stim_scr, *, tc):
    i = pl.program_id(0)
    n_pairs = lbre_ref.shape[0]

    @pl.when(i == 0)
    def _():
        stre_scr[...] = h0re_ref[...]
        stim_scr[...] = h0im_ref[...]

    pitch = tc + STRIDE_PAD
    _s5_input_phase(u_ref, bq_ref, bre_scr, bim_scr, tc, pitch, False)

    lr = lbre_ref[...]
    li = lbim_ref[...]

    def step(t, carry):
        hr, hi = carry
        idx = pl.ds(t, n_pairs, stride=pitch)
        nr = lr * hr - li * hi + bre_scr[idx, :]
        ni = lr * hi + li * hr + bim_scr[idx, :]
        bre_scr[idx, :] = nr
        bim_scr[idx, :] = ni
        return nr, ni

    hr, hi = lax.fori_loop(0, tc, step, (stre_scr[...], stim_scr[...]))
    stre_scr[...] = hr
    stim_scr[...] = hi
    hre_out[...] = hr
    him_out[...] = hi

    _s5_output_phase(u_ref, cq_ref, dskip_ref, y_ref, bre_scr, bim_scr, tc, pitch, False)


def s5_prompt(u, col0, bq, cq, lb_re, lb_im, d_skip, h0_re, h0_im, tc=256):
    t = u.shape[0]
    n_pairs = lb_re.shape[0]
    d = n_pairs * 2 * GROUP_CH
    tc = min(tc, t)
    nq = bq.shape[0]
    whole = lambda *shape: pl.BlockSpec(shape, lambda i: (0,) * len(shape))
    sd = jax.ShapeDtypeStruct
    return pl.pallas_call(
        functools.partial(_s5_prompt_kernel, tc=tc),
        grid=(t // tc,),
        in_specs=[pl.BlockSpec((tc, d), lambda i: (i, col0 // d)),
                  whole(*bq.shape), whole(*cq.shape),
                  whole(n_pairs, LANES), whole(n_pairs, LANES), whole(1, d),
                  whole(n_pairs, LANES), whole(n_pairs, LANES)],
        out_specs=[pl.BlockSpec((tc, d), lambda i: (i, 0)),
                   whole(n_pairs, LANES), whole(n_pairs, LANES)],
        out_shape=[sd((t, d), F32), sd((n_pairs, LANES), F32), sd((n_pairs, LANES), F32)],
        scratch_shapes=[pltpu.VMEM((n_pairs * (tc + STRIDE_PAD), LANES), F32),
                        pltpu.VMEM((n_pairs * (tc + STRIDE_PAD), LANES), F32),
                        pltpu.VMEM((n_pairs, LANES), F32), pltpu.VMEM((n_pairs, LANES), F32)],
        compiler_params=_params("arbitrary"),
        name="s5_prompt",
    )(u, bq.astype(BF16), cq.astype(BF16), lb_re, lb_im, d_skip.reshape(1, d), h0_re, h0_im)


def _s5_sample_kernel(u_ref, bq_ref, cq_ref, lbre_ref, lbim_ref, dskip_ref, h0re_ref, h0im_ref,
                      y_ref, hre_out, him_out, bre_scr, bim_scr, *, nt, nb):
    rows = nt * nb
    n_pairs = lbre_ref.shape[0]
    _s5_input_phase(u_ref, bq_ref, bre_scr, bim_scr, rows, rows, True)

    def pair_body(pr, carry):
        lr = lbre_ref[pl.ds(pr, 1), :]
        li = lbim_ref[pl.ds(pr, 1), :]
        hr = h0re_ref[pr]
        hi = h0im_ref[pr]
        for t in range(nt):
            idx = pl.ds(pl.multiple_of(pr * rows + t * nb, 8), nb)
            nr = lr * hr - li * hi + bre_scr[idx, :]
            ni = lr * hi + li * hr + bim_scr[idx, :]
            bre_scr[idx, :] = nr
            bim_scr[idx, :] = ni
            hr, hi = nr, ni
        hre_out[pr] = hr
        him_out[pr] = hi
        return carry

    lax.fori_loop(0, n_pairs, pair_body, 0)
    _s5_output_phase(u_ref, cq_ref, dskip_ref, y_ref, bre_scr, bim_scr, rows, rows, False)


def s5_sample(u, bq, cq, lb_re, lb_im, d_skip, h0_re, h0_im, nt, nb):
    rows, d = u.shape
    n_pairs = lb_re.shape[0]
    sd = jax.ShapeDtypeStruct
    return pl.pallas_call(
        functools.partial(_s5_sample_kernel, nt=nt, nb=nb),
        out_shape=[sd((rows, d), F32), sd((n_pairs, nb, LANES), F32), sd((n_pairs, nb, LANES), F32)],
        scratch_shapes=[pltpu.VMEM((n_pairs * rows, LANES), F32), pltpu.VMEM((n_pairs * rows, LANES), F32)],
        compiler_params=pltpu.CompilerParams(vmem_limit_bytes=VMEM_LIMIT),
        name="s5_sample",
    )(u, bq, cq.astype(BF16), lb_re, lb_im, d_skip.reshape(1, d), h0_re, h0_im)


def _rms(x, g):
    ms = jnp.mean(x * x, axis=-1, keepdims=True)
    return x * lax.rsqrt(ms + EPS) * g


def _mix_kernel(att_ref, y_ref, ga_ref, gs_ref, wglu_ref, bglu_ref, gatt_ref, gssm_ref,
                attn_ref, ssmn_ref):
    attn_ref[...] = (_rms(att_ref[...], gatt_ref[...]) * jax.nn.silu(ga_ref[...])).astype(attn_ref.dtype)
    y = jax.nn.gelu(y_ref[...])
    y = y * jax.nn.sigmoid(_dot(y.astype(BF16), wglu_ref[...]) + bglu_ref[...])
    ssmn_ref[...] = (_rms(y, gssm_ref[...]) * jax.nn.silu(gs_ref[...])).astype(ssmn_ref.dtype)


def mix_post(att, y, gug, w_glu, b_glu, g_att, g_ssm, tm=256):
    m, d = att.shape
    tm = min(tm, m)
    row = lambda c: pl.BlockSpec((tm, d), lambda i: (i, c))
    vec = pl.BlockSpec((1, d), lambda i: (0, 0))
    return pl.pallas_call(
        _mix_kernel,
        grid=(m // tm,),
        in_specs=[row(0), row(0), row(0), row(2),
                  pl.BlockSpec((d, d), lambda i: (0, 0)), vec, vec, vec],
        out_specs=[row(0), row(0)],
        out_shape=[jax.ShapeDtypeStruct((m, d), BF16), jax.ShapeDtypeStruct((m, d), BF16)],
        compiler_params=_params("parallel"),
        name="mix_post",
    )(att, y, gug, gug, w_glu, b_glu.reshape(1, d), g_att.reshape(1, d), g_ssm.reshape(1, d))


def _out_kernel(x_ref, a_ref, s_ref, wa_ref, ws_ref, g_ref, o_ref, xn_ref, rows_scr, ssq_scr, *, tn):
    j = pl.program_id(1)
    h = x_ref[...] + (_dot(a_ref[...], wa_ref[...]) + _dot(s_ref[...], ws_ref[...]))
    o_ref[...] = h
    _row_norm_collect(j, h, rows_scr, ssq_scr)

    @pl.when(j == pl.num_programs(1) - 1)
    def _():
        _row_norm_finish(rows_scr, ssq_scr, g_ref, xn_ref, tn)


def out_project(x, att_n, ssm_n, w_out, g_next, tm=512, tn=1024):
    m, dm = x.shape
    dh = att_n.shape[1]
    tm = min(tm, m)
    tn = min(tn, dm)
    return pl.pallas_call(
        functools.partial(_out_kernel, tn=tn),
        grid=(m // tm, dm // tn),
        in_specs=[pl.BlockSpec((tm, tn), lambda i, j: (i, j)),
                  pl.BlockSpec((tm, dh), lambda i, j: (i, 0)),
                  pl.BlockSpec((tm, dh), lambda i, j: (i, 0)),
                  pl.BlockSpec((dh, tn), lambda i, j: (0, j)),
                  pl.BlockSpec((dh, tn), lambda i, j: (1, j)),
                  pl.BlockSpec((1, dm), lambda i, j: (0, 0))],
        out_specs=[pl.BlockSpec((tm, tn), lambda i, j: (i, j)),
                   pl.BlockSpec((tm, dm), lambda i, j: (i, 0))],
        out_shape=[jax.ShapeDtypeStruct((m, dm), F32), jax.ShapeDtypeStruct((m, dm), BF16)],
        scratch_shapes=[pltpu.VMEM((dm // tn, tm, tn), F32), pltpu.VMEM((tm, 1), F32)],
        compiler_params=_params("parallel", "arbitrary"),
        name="out_proj",
    )(x, att_n, ssm_n, w_out, w_out, g_next.reshape(1, dm))


def _row_norm_finish(rows_scr, ssq_scr, g_ref, o_ref, tn):
    d = o_ref.shape[1]
    inv = lax.rsqrt(ssq_scr[...] * (1.0 / d) + EPS)
    for jj in range(d // tn):
        sl = slice(jj * tn, (jj + 1) * tn)
        o_ref[:, sl] = (rows_scr[jj] * inv * g_ref[:, sl]).astype(o_ref.dtype)


def _row_norm_collect(j, val, rows_scr, ssq_scr):
    part = jnp.sum(val * val, axis=-1, keepdims=True)

    @pl.when(j == 0)
    def _():
        ssq_scr[...] = part

    @pl.when(j > 0)
    def _():
        ssq_scr[...] += part

    rows_scr[j] = val


def _ple_kernel(h_ref, xn_ref, p_ref, wg_ref, wp_ref, gf_ref, o_ref, rows_scr, ssq_scr, *, tn):
    j = pl.program_id(1)
    gate = jax.nn.sigmoid(_dot(xn_ref[...], wg_ref[...]))
    ple = _dot(p_ref[...].astype(BF16), wp_ref[...])
    _row_norm_collect(j, h_ref[...] + ple * gate, rows_scr, ssq_scr)

    @pl.when(j == pl.num_programs(1) - 1)
    def _():
        _row_norm_finish(rows_scr, ssq_scr, gf_ref, o_ref, tn)


def ple_gate_final(h, xn, p, w_gate, w_ple, g_final, tm=512, tn=512):
    m, dm = h.shape
    dp = p.shape[1]
    tm = min(tm, m)
    tn = min(tn, dm)
    return pl.pallas_call(
        functools.partial(_ple_kernel, tn=tn),
        grid=(m // tm, dm // tn),
        in_specs=[pl.BlockSpec((tm, tn), lambda i, j: (i, j)),
                  pl.BlockSpec((tm, dm), lambda i, j: (i, 0)),
                  pl.BlockSpec((tm, dp), lambda i, j: (i, 0)),
                  pl.BlockSpec((dm, tn), lambda i, j: (0, j)),
                  pl.BlockSpec((dp, tn), lambda i, j: (0, j)),
                  pl.BlockSpec((1, dm), lambda i, j: (0, 0))],
        out_specs=pl.BlockSpec((tm, dm), lambda i, j: (i, 0)),
        out_shape=jax.ShapeDtypeStruct((m, dm), F32),
        scratch_shapes=[pltpu.VMEM((dm // tn, tm, tn), F32), pltpu.VMEM((tm, 1), F32)],
        compiler_params=_params("parallel", "arbitrary"),
        name="ple_gate_final",
    )(h, xn, p, w_gate, w_ple, g_final.reshape(1, dm))


def _pair_layout(x):
    return x.reshape(*x.shape[:-2], x.shape[-2] // 2, 2 * x.shape[-1])


def kernel(x_prompt, x_sample, cache_k, cache_v, state_ssm_re, state_ssm_im, page_table, p_prompt, p_sample, g_in, w_in, att_bias, a_re, a_im, log_dt, b_re, b_im, c_re, c_im, d_skip, w_glu, b_glu, g_att, g_ssm, w_out, g_ple, w_ple, w_ple_gate, g_final):
    depth = w_in.shape[0]
    assert depth == 1, "single-layer step"
    nb_p, t_p, dm = x_prompt.shape
    assert nb_p == 1, "one prompt sequence"
    nb, nt, _ = x_sample.shape
    n_groups, n_state = a_re.shape[1:]
    d_ssm = n_groups * GROUP_CH
    d_att = dm - d_ssm
    nh = d_att // HEAD_DIM
    page = cache_k.shape[2]
    scale = 1.0 / math.sqrt(HEAD_DIM)

    w_in_b = w_in[0].astype(BF16)
    w_glu_b = w_glu[0].astype(BF16)
    w_out_b = w_out[0].astype(BF16)
    w_ple_b = w_ple[0].astype(BF16)
    w_gate_b = w_ple_gate[0].astype(BF16)

    lb_re, lb_im, bb_re, bb_im = s5_discretize(a_re[0], a_im[0], log_dt[0], b_re[0], b_im[0])
    bq, cq = _s5_block_operands(bb_re, bb_im, c_re[0], c_im[0])
    lb_re_p, lb_im_p = _pair_layout(lb_re), _pair_layout(lb_im)
    dsk = d_skip[0].reshape(-1)

    def in_proj(x2d):
        xn = rmsnorm_rows(x2d, g_in[0], BF16)
        (q,) = project(xn, w_in_b, 0, d_att, [BF16], scale=scale)
        k, k_b = project(xn, w_in_b, d_att, d_att, [F32, BF16])
        v, v_b = project(xn, w_in_b, 2 * d_att, d_att, [F32, BF16])
        (gug,) = project(xn, w_in_b, 3 * d_att, d_att + 2 * d_ssm, [F32])
        return q, k, k_b, v, v_b, gug

    def tail(x2d, p2d, att, y, gug):
        att_n, ssm_n = mix_post(att, y, gug, w_glu_b, b_glu[0], g_att[0], g_ssm[0])
        h2, xn2 = out_project(x2d, att_n, ssm_n, w_out_b, g_ple[0])
        return ple_gate_final(h2, xn2, p2d, w_gate_b, w_ple_b, g_final)

    xp = x_prompt.reshape(t_p, dm)
    q, k, k_b, v, v_b, gug = in_proj(xp)
    att = attention_prompt(q, k_b, v_b, att_bias[0])
    zero_state = jnp.zeros((n_groups // 2, 2 * n_state), F32)
    y, hre_p, him_p = s5_prompt(gug, d_att, bq, cq, lb_re_p, lb_im_p, dsk, zero_state, zero_state)
    y_prompt = tail(xp, p_prompt[0, 0], att, y, gug).reshape(1, t_p, dm)
    new_k_prompt = k.reshape(1, 1, t_p, nh, HEAD_DIM)
    new_v_prompt = v.reshape(1, 1, t_p, nh, HEAD_DIM)
    new_re_prompt = hre_p.reshape(1, 1, n_groups, n_state)
    new_im_prompt = him_p.reshape(1, 1, n_groups, n_state)

    xs = x_sample.reshape(nb * nt, dm)
    q, k, _, v, _, gug = in_proj(xs)
    att = attention_sample(
        q.reshape(nb, nt, d_att), k.reshape(nb, nt, d_att), v.reshape(nb, nt, d_att),
        cache_k[0], cache_v[0], page_table, att_bias[0]).reshape(nb * nt, d_att)
    u_tb = gug[:, d_att:d_att + d_ssm].reshape(nb, nt, d_ssm).transpose(1, 0, 2).reshape(nt * nb, d_ssm)
    h0_re = _pair_layout(state_ssm_re[0]).transpose(1, 0, 2)
    h0_im = _pair_layout(state_ssm_im[0]).transpose(1, 0, 2)
    y_tb, hre_s, him_s = s5_sample(u_tb, bq, cq, lb_re_p, lb_im_p, dsk, h0_re, h0_im, nt, nb)
    y = y_tb.reshape(nt, nb, d_ssm).transpose(1, 0, 2).reshape(nb * nt, d_ssm)
    y_sample = tail(xs, p_sample[0].reshape(nb * nt, -1), att, y, gug).reshape(nb, nt, dm)
    new_k_sample = k.reshape(1, nb, nt, nh, HEAD_DIM)
    new_v_sample = v.reshape(1, nb, nt, nh, HEAD_DIM)
    new_re_sample = hre_s.transpose(1, 0, 2).reshape(1, nb, n_groups, n_state)
    new_im_sample = him_s.transpose(1, 0, 2).reshape(1, nb, n_groups, n_state)

    return (y_prompt, y_sample, new_k_prompt, new_v_prompt, new_k_sample, new_v_sample,
            new_re_prompt, new_im_prompt, new_re_sample, new_im_sample)
```

```python
import functools
import math

import jax
import jax.numpy as jnp
from jax import lax
from jax.experimental import pallas as pl
from jax.experimental.pallas import tpu as pltpu

F32 = jnp.float32
BF16 = jnp.bfloat16

HEAD_DIM = 128
GROUP_CH = 16
STATE_N = 64
EPS = 1e-6
LANES = 128
VMEM_LIMIT = 56 * 1024 * 1024
GROUPS_PER_QUAD = LANES // GROUP_CH
PAIRS_PER_QUAD = GROUPS_PER_QUAD // 2
SUBLANES = 8
HEAD_BLOCK = SUBLANES
STRIDE_PAD = 4


def _params(*sem):
    return pltpu.CompilerParams(dimension_semantics=sem, vmem_limit_bytes=VMEM_LIMIT)


def _dot(a, b):
    return jnp.dot(a, b, preferred_element_type=F32)


def _split_bf16(x):
    hi = x.astype(BF16)
    lo = (x - hi.astype(F32)).astype(BF16)
    return hi, lo


def _dot_x3(a, b):
    ah, al = _split_bf16(a)
    bh, bl = _split_bf16(b)
    return _dot(ah, bh) + _dot(al, bh) + _dot(ah, bl)


def _softplus(z):
    return jnp.maximum(z, 0.0) + jnp.log(1.0 + jnp.exp2(jnp.abs(z) * -math.log2(math.e)))


def _rms_kernel(x_ref, g_ref, o_ref):
    x = x_ref[...]
    ms = jnp.mean(x * x, axis=-1, keepdims=True)
    o_ref[...] = (x * lax.rsqrt(ms + EPS) * g_ref[...]).astype(o_ref.dtype)


def rmsnorm_rows(x, g, out_dtype, tm=256):
    m, d = x.shape
    tm = min(tm, m)
    return pl.pallas_call(
        _rms_kernel,
        grid=(m // tm,),
        in_specs=[pl.BlockSpec((tm, d), lambda i: (i, 0)),
                  pl.BlockSpec((1, d), lambda i: (0, 0))],
        out_specs=pl.BlockSpec((tm, d), lambda i: (i, 0)),
        out_shape=jax.ShapeDtypeStruct((m, d), out_dtype),
        compiler_params=_params("parallel"),
        name="rmsnorm",
    )(x, g.reshape(1, d))


def _proj_kernel(x_ref, w_ref, *o_refs, scale):
    acc = _dot(x_ref[...], w_ref[...])
    if scale != 1.0:
        acc = acc * scale
    for o_ref in o_refs:
        o_ref[...] = acc.astype(o_ref.dtype)


def project(xn, w, col0, n, out_dtypes, scale=1.0, tm=1024, tn=1024):
    m, k = xn.shape
    tm = min(tm, m)
    tn = math.gcd(tn, n, col0)
    off = col0 // tn
    return pl.pallas_call(
        functools.partial(_proj_kernel, scale=scale),
        grid=(m // tm, n // tn),
        in_specs=[pl.BlockSpec((tm, k), lambda i, j: (i, 0)),
                  pl.BlockSpec((k, tn), lambda i, j: (0, off + j))],
        out_specs=[pl.BlockSpec((tm, tn), lambda i, j: (i, j)) for _ in out_dtypes],
        out_shape=[jax.ShapeDtypeStruct((m, n), dt) for dt in out_dtypes],
        compiler_params=_params("parallel", "arbitrary"),
        name="in_proj",
    )(xn, w)


def _tail_matrix(tk, row_sums, split):
    j = jnp.arange(tk)[:, None]
    s = jnp.arange(tk)[None, :]
    u = (j > s).astype(BF16)
    if row_sums:
        u = jnp.concatenate([u, jnp.ones((tk, LANES), BF16)], axis=1)
    return jnp.concatenate([u, u], axis=0) if split else u


def _suffix_sums(sp, uo):
    if uo.shape[0] == 2 * sp.shape[1]:
        return _dot(jnp.concatenate(_split_bf16(sp), axis=1), uo)
    return _dot(sp.astype(BF16), uo)


def _sb_tile(q, k, v, bias, uo, r, mask, tk):
    s = lax.dot_general(q, k, (((1,), (1,)), ((), ())), preferred_element_type=F32)
    z = s + bias
    sp = _softplus(z)
    if mask is not None:
        sp = jnp.where(mask, sp, 0.0)
    t2 = _suffix_sums(sp, uo)
    rb = jnp.concatenate([r] * (tk // LANES), axis=1) if tk > LANES else r
    w = jnp.exp(z - sp - t2[:, :tk] - rb)
    if mask is not None:
        w = jnp.where(mask, w, 0.0)
    if uo.shape[1] > tk:
        rs = t2[:, tk:]
    else:
        rs = jnp.broadcast_to(jnp.sum(sp, axis=1, keepdims=True), r.shape)
    return _dot(w.astype(BF16), v), rs


def _attn_kernel(bias_ref, q_ref, k_ref, v_ref, uo_ref, o_ref, acc_ref, r_ref, *, tq, tk):
    h = pl.program_id(0)
    qi = pl.program_id(1)
    bias = bias_ref[h]
    q = q_ref[...]
    uo = uo_ref[...]
    acc_ref[...] = jnp.zeros_like(acc_ref)
    r_ref[...] = jnp.zeros_like(r_ref)
    nd = tq // tk

    def tile(kt, row0, mask):
        start = pl.multiple_of(kt * tk, tk)
        k = k_ref[pl.ds(start, tk), :]
        v = v_ref[pl.ds(start, tk), :]
        r = r_ref[row0:, :]
        pv, rs = _sb_tile(q[row0:, :], k, v, bias, uo, r, mask, tk)
        acc_ref[row0:, :] += pv
        r_ref[row0:, :] = r + rs

    for d in range(nd - 1, -1, -1):
        row0 = d * tk
        row = lax.broadcasted_iota(jnp.int32, (tq - row0, tk), 0)
        col = lax.broadcasted_iota(jnp.int32, (tq - row0, tk), 1)
        tile(qi * nd + d, row0, col < row)

    def body(i, carry):
        for d in range(nd):
            tile((qi - i) * nd - 1 - d, 0, None)
        return carry

    lax.fori_loop(0, qi, body, 0)
    o_ref[...] = acc_ref[...]


def attention_prompt(q, k, v, bias, tq=2048, tk=256):
    t, d = q.shape
    nh = d // HEAD_DIM
    tq = min(tq, t)
    tk = min(tk, tq)
    return pl.pallas_call(
        functools.partial(_attn_kernel, tq=tq, tk=tk),
        grid=(nh, t // tq),
        in_specs=[pl.BlockSpec(memory_space=pltpu.SMEM),
                  pl.BlockSpec((tq, HEAD_DIM), lambda h, i: (i, h)),
                  pl.BlockSpec((t, HEAD_DIM), lambda h, i: (0, h)),
                  pl.BlockSpec((t, HEAD_DIM), lambda h, i: (0, h)),
                  pl.BlockSpec((tk, tk), lambda h, i: (0, 0))],
        out_specs=pl.BlockSpec((tq, HEAD_DIM), lambda h, i: (i, h)),
        out_shape=jax.ShapeDtypeStruct((t, d), F32),
        scratch_shapes=[pltpu.VMEM((tq, HEAD_DIM), F32), pltpu.VMEM((tq, LANES), F32)],
        compiler_params=_params("parallel", "arbitrary"),
        name="sb_attention_prompt",
    )(bias, q, k, v, _tail_matrix(tk, row_sums=False, split=False))


def _dec_attn_kernel(pt_ref, q_ref, bias_ref, knew_ref, vnew_ref, *rest, nt, nh, page, gp):
    del pt_ref
    nblk = nh // HEAD_BLOCK
    npb = gp * nblk
    flat = lambda ref: ref.reshape(page * HEAD_BLOCK, HEAD_DIM)
    k_pages = [[flat(r) for r in rest[g * nblk:(g + 1) * nblk]] for g in range(gp)]
    v_pages = [[flat(r) for r in rest[npb + g * nblk:npb + (g + 1) * nblk]] for g in range(gp)]
    uo_ref, o_ref, acc_ref, r_ref, kpad_ref, vpad_ref = rest[2 * npb:]
    j = pl.program_id(1)
    rows = nh * nt
    hpg = SUBLANES // nt
    ng = rows // SUBLANES
    bias = bias_ref[...]
    uo = uo_ref[...]
    sub_head = lax.broadcasted_iota(jnp.int32, (SUBLANES, page), 0) // nt

    def head_rows(blocks, g):
        parts = []
        for p in range(hpg):
            blk, hb = divmod(g * hpg + p, HEAD_BLOCK)
            parts.append(blocks[blk][pl.ds(hb, page, stride=HEAD_BLOCK), :])
        return jnp.concatenate(parts, axis=0).astype(BF16)

    def process(k_ref, v_ref, mask):
        zs = []
        for g in range(ng):
            qg = q_ref[g * SUBLANES:(g + 1) * SUBLANES, :].astype(BF16)
            s = lax.dot_general(qg, head_rows(k_ref, g), (((1,), (1,)), ((), ())),
                                preferred_element_type=F32)
            z = s[:, :page]
            for p in range(1, hpg):
                z = jnp.where(sub_head == p, s[:, p * page:(p + 1) * page], z)
            zs.append(z)
        z = jnp.concatenate(zs, axis=0) + bias
        sp = _softplus(z)
        if mask is not None:
            sp = jnp.where(mask, sp, 0.0)
        t2 = _suffix_sums(sp, uo)
        r = r_ref[...]
        w = jnp.exp(z - sp - t2[:, :page] - r)
        if mask is not None:
            w = jnp.where(mask, w, 0.0)
        for g in range(ng):
            wg = w[g * SUBLANES:(g + 1) * SUBLANES, :]
            wexp = jnp.concatenate([jnp.where(sub_head == p, wg, 0.0) for p in range(hpg)],
                                   axis=1).astype(BF16)
            acc_ref[g * SUBLANES:(g + 1) * SUBLANES, :] += _dot(wexp, head_rows(v_ref, g))
        r_ref[...] = r + t2[:, page:]

    @pl.when(j == 0)
    def _():
        acc_ref[...] = jnp.zeros_like(acc_ref)
        r_ref[...] = jnp.zeros_like(r_ref)
        kpad_ref[...] = jnp.zeros_like(kpad_ref)
        vpad_ref[...] = jnp.zeros_like(vpad_ref)
        kpad_ref[:, 0:nt * HEAD_BLOCK, :] = knew_ref[...]
        vpad_ref[:, 0:nt * HEAD_BLOCK, :] = vnew_ref[...]
        tok = lax.broadcasted_iota(jnp.int32, (rows, page), 0) % nt
        key = lax.broadcasted_iota(jnp.int32, (rows, page), 1)
        process([kpad_ref.at[bk] for bk in range(nblk)],
                [vpad_ref.at[bk] for bk in range(nblk)], key < tok)

    @pl.when(j > 0)
    def _():
        for g in range(gp):
            process(k_pages[g], v_pages[g], None)

    @pl.when(j == pl.num_programs(1) - 1)
    def _():
        o_ref[...] = acc_ref[...]


def attention_sample(q, k_new, v_new, cache_k, cache_v, page_table, bias, pages_per_step=8):
    b, nt, d = q.shape
    nh = d // HEAD_DIM
    n_pool, page = cache_k.shape[:2]
    n_pages = page_table.shape[1]
    rows = nh * nt
    gp = math.gcd(n_pages, pages_per_step)
    assert SUBLANES % nt == 0 and nh % HEAD_BLOCK == 0
    nblk = nh // HEAD_BLOCK
    q_rows = q.astype(F32).reshape(b, nt, nh, HEAD_DIM).transpose(0, 2, 1, 3).reshape(b, rows, HEAD_DIM)
    bias_rows = jnp.broadcast_to(jnp.repeat(bias, nt)[:, None], (rows, LANES)).astype(F32)
    cache_k = cache_k.reshape(n_pool, page, nh, HEAD_DIM)
    cache_v = cache_v.reshape(n_pool, page, nh, HEAD_DIM)
    blocked = lambda x: x.reshape(b, nt, nblk, HEAD_BLOCK, HEAD_DIM).transpose(0, 2, 1, 3, 4).reshape(
        b, nblk, nt * HEAD_BLOCK, HEAD_DIM)

    def page_spec(g, bk):
        def index_map(i, j, pt):
            return (pt[i, n_pages - 1 - (jnp.maximum(j, 1) - 1) * gp - g], 0, bk, 0)
        return pl.BlockSpec((None, page, HEAD_BLOCK, HEAD_DIM), index_map)

    page_specs = [page_spec(g, bk) for g in range(gp) for bk in range(nblk)]
    per_seq = lambda *shape: pl.BlockSpec((None, *shape), lambda i, j, pt: (i,) + (0,) * len(shape))
    pad_scratch = pltpu.VMEM((nblk, page * HEAD_BLOCK, HEAD_DIM), F32)
    grid_spec = pltpu.PrefetchScalarGridSpec(
        num_scalar_prefetch=1,
        grid=(b, n_pages // gp + 1),
        in_specs=[per_seq(rows, HEAD_DIM),
                  pl.BlockSpec((rows, LANES), lambda i, j, pt: (0, 0)),
                  per_seq(nblk, nt * HEAD_BLOCK, HEAD_DIM), per_seq(nblk, nt * HEAD_BLOCK, HEAD_DIM),
                  *page_specs, *page_specs,
                  pl.BlockSpec((2 * page, page + LANES), lambda i, j, pt: (0, 0))],
        out_specs=per_seq(rows, HEAD_DIM),
        scratch_shapes=[pltpu.VMEM((rows, HEAD_DIM), F32), pltpu.VMEM((rows, LANES), F32),
                        pad_scratch, pad_scratch],
    )
    out = pl.pallas_call(
        functools.partial(_dec_attn_kernel, nt=nt, nh=nh, page=page, gp=gp),
        grid_spec=grid_spec,
        out_shape=jax.ShapeDtypeStruct((b, rows, HEAD_DIM), F32),
        compiler_params=_params("parallel", "arbitrary"),
        name="sb_attention_sample",
    )(page_table, q_rows, bias_rows, blocked(k_new), blocked(v_new),
      *([cache_k] * (gp * nblk)), *([cache_v] * (gp * nblk)), _tail_matrix(page, row_sums=True, split=True))
    return out.reshape(b, nh, nt, HEAD_DIM).transpose(0, 2, 1, 3).reshape(b, nt, d)


def _s5_discretize_kernel(are_ref, aim_ref, ldt_ref, bre_ref, bim_ref,
                          lbre_ref, lbim_ref, bbre_ref, bbim_ref):
    a_re = are_ref[...]
    a_im = aim_ref[...]
    dt = jnp.exp(ldt_ref[...])
    mag = jnp.exp(a_re * dt)
    ang = a_im * dt
    lb_re = mag * jnp.cos(ang)
    lb_im = mag * jnp.sin(ang)
    den = a_re * a_re + a_im * a_im
    num_re = lb_re - 1.0
    f_re = (num_re * a_re + lb_im * a_im) / den
    f_im = (lb_im * a_re - num_re * a_im) / den
    lbre_ref[...] = lb_re
    lbim_ref[...] = lb_im
    b_re = bre_ref[...]
    b_im = bim_ref[...]
    bbre_ref[...] = f_re * b_re - f_im * b_im
    bbim_ref[...] = f_re * b_im + f_im * b_re


def s5_discretize(a_re, a_im, log_dt, b_re, b_im):
    g, n = a_re.shape
    p = b_re.shape[-1]
    rep = lambda x: jnp.repeat(x, p, axis=0)
    rows = lambda x: x.transpose(0, 2, 1).reshape(g * p, n)
    sd = jax.ShapeDtypeStruct((g * p, n), F32)
    lb_re, lb_im, bb_re, bb_im = pl.pallas_call(
        _s5_discretize_kernel,
        out_shape=[sd, sd, sd, sd],
        name="s5_discretize",
    )(rep(a_re), rep(a_im), rep(jnp.broadcast_to(log_dt[:, None], (g, n))), rows(b_re), rows(b_im))
    return lb_re[::p], lb_im[::p], bb_re.reshape(g, p, n), bb_im.reshape(g, p, n)


def _s5_block_operands(bb_re, bb_im, c_re, c_im):
    g, p, n = bb_re.shape
    nq = g // GROUPS_PER_QUAD
    eye = jnp.eye(GROUPS_PER_QUAD, dtype=F32)

    def b_side(x):
        x = x.reshape(nq, GROUPS_PER_QUAD, p, n)
        return (x[:, :, :, None, :] * eye[None, :, None, :, None]).reshape(
            nq, GROUPS_PER_QUAD * p, GROUPS_PER_QUAD * n)

    def c_side(x):
        x = x.reshape(nq, GROUPS_PER_QUAD, p, n).transpose(0, 1, 3, 2)
        return (x[:, :, :, None, :] * eye[None, :, None, :, None]).reshape(
            nq, GROUPS_PER_QUAD * n, GROUPS_PER_QUAD * p)

    bq = jnp.concatenate([b_side(bb_re), b_side(bb_im)], axis=2)
    cq = jnp.concatenate([c_side(c_re), -c_side(c_im)], axis=1)
    return bq, cq


def _s5_input_phase(u_ref, bq_ref, bre_scr, bim_scr, rows, pitch, precise):
    nq = bq_ref.shape[0]
    half = PAIRS_PER_QUAD * LANES
    for qd in range(nq):
        _s5_input_quad(u_ref, bq_ref, bre_scr, bim_scr, qd, rows, pitch, precise)


def _s5_input_quad(u_ref, bq_ref, bre_scr, bim_scr, qd, rows, pitch, precise):
    half = PAIRS_PER_QUAD * LANES
    uq = u_ref[:, qd * LANES:(qd + 1) * LANES]
    if precise:
        bu = _dot_x3(uq, bq_ref[qd])
    else:
        bu = _dot(uq.astype(BF16), bq_ref[qd])
    for jj in range(PAIRS_PER_QUAD):
        r0 = (PAIRS_PER_QUAD * qd + jj) * pitch
        bre_scr[r0:r0 + rows, :] = bu[:, jj * LANES:(jj + 1) * LANES]
        bim_scr[r0:r0 + rows, :] = bu[:, half + jj * LANES:half + (jj + 1) * LANES]


def _s5_output_phase(u_ref, cq_ref, dskip_ref, y_ref, hre_scr, him_scr, rows, pitch, precise):
    for qd in range(cq_ref.shape[0]):
        _s5_output_quad(u_ref, cq_ref, dskip_ref, y_ref, hre_scr, him_scr, qd, rows, pitch, precise)


def _s5_output_quad(u_ref, cq_ref, dskip_ref, y_ref, hre_scr, him_scr, qd, rows, pitch, precise):
    parts = []
    for scr in (hre_scr, him_scr):
        for jj in range(PAIRS_PER_QUAD):
            r0 = (PAIRS_PER_QUAD * qd + jj) * pitch
            parts.append(scr[r0:r0 + rows, :])
    hcat = jnp.concatenate(parts, axis=1)
    if precise:
        yq = _dot_x3(hcat, cq_ref[qd])
    else:
        yq = _dot(hcat.astype(BF16), cq_ref[qd])
    sl = slice(qd * LANES, (qd + 1) * LANES)
    y_ref[:, sl] = yq + dskip_ref[:, sl] * u_ref[:, sl]


S5_STAGES = 3


def _s5_prompt_kernel(ua_ref, uc_ref, bq_ref, cq_ref, lbre_ref, lbim_ref, dskip_ref, h0re_ref,
                      h0im_ref, y_ref, hre_out, him_out, *scratch, tc, n_chunks):
    bufs_re, bufs_im = scratch[:S5_STAGES], scratch[S5_STAGES:2 * S5_STAGES]
    stre_scr, stim_scr = scratch[2 * S5_STAGES:]
    i = pl.program_id(0)
    n_pairs = lbre_ref.shape[0]
    nq = bq_ref.shape[0]
    pitch = tc + STRIDE_PAD
    steps_per_quad = tc // nq

    @pl.when(i == 0)
    def _():
        for buf in scratch[:2 * S5_STAGES]:
            buf[...] = jnp.zeros_like(buf)
        stre_scr[...] = h0re_ref[...]
        stim_scr[...] = h0im_ref[...]

    def body(a_re, a_im, s_re, s_im, c_re, c_im):
        lr = lbre_ref[...]
        li = lbim_ref[...]
        hr0, hi0 = stre_scr[...], stim_scr[...]
        hr, hi = hr0, hi0
        for qd in range(nq):
            _s5_input_quad(ua_ref, bq_ref, a_re, a_im, qd, tc, pitch, False)
            for t in range(qd * steps_per_quad, (qd + 1) * steps_per_quad):
                idx = pl.ds(t, n_pairs, stride=pitch)
                hr, hi = (lr * hr - li * hi + s_re[idx, :], lr * hi + li * hr + s_im[idx, :])
                s_re[idx, :] = hr
                s_im[idx, :] = hi
            _s5_output_quad(uc_ref, cq_ref, dskip_ref, y_ref, c_re, c_im, qd, tc, pitch, False)
        scanned = jnp.logical_and(i >= 1, i <= n_chunks)
        stre_scr[...] = jnp.where(scanned, hr, hr0)
        stim_scr[...] = jnp.where(scanned, hi, hi0)

        @pl.when(i == n_chunks)
        def _():
            hre_out[...] = hr
            him_out[...] = hi

    for r in range(S5_STAGES):
        a, s, c = r, (r + 2) % S5_STAGES, (r + 1) % S5_STAGES

        @pl.when(lax.rem(i, S5_STAGES) == r)
        def _(a=a, s=s, c=c):
            body(bufs_re[a], bufs_im[a], bufs_re[s], bufs_im[s], bufs_re[c], bufs_im[c])


def s5_prompt(u, col0, bq, cq, lb_re, lb_im, d_skip, h0_re, h0_im, tc=128):
    t = u.shape[0]
    n_pairs = lb_re.shape[0]
    d = n_pairs * 2 * GROUP_CH
    tc = min(tc, t)
    n_chunks = t // tc
    assert tc % bq.shape[0] == 0
    last = n_chunks - 1
    whole = lambda *shape: pl.BlockSpec(shape, lambda i: (0,) * len(shape))
    chunk = lambda lag, col: pl.BlockSpec(
        (tc, d), lambda i: (jnp.clip(i - lag, 0, last), col))
    buf = pltpu.VMEM((n_pairs * (tc + STRIDE_PAD), LANES), F32)
    state = pltpu.VMEM((n_pairs, LANES), F32)
    sd = jax.ShapeDtypeStruct
    return pl.pallas_call(
        functools.partial(_s5_prompt_kernel, tc=tc, n_chunks=n_chunks),
        grid=(n_chunks + S5_STAGES - 1,),
        in_specs=[chunk(0, col0 // d), chunk(S5_STAGES - 1, col0 // d),
                  whole(*bq.shape), whole(*cq.shape),
                  whole(n_pairs, LANES), whole(n_pairs, LANES), whole(1, d),
                  whole(n_pairs, LANES), whole(n_pairs, LANES)],
        out_specs=[chunk(S5_STAGES - 1, 0), whole(n_pairs, LANES), whole(n_pairs, LANES)],
        out_shape=[sd((t, d), F32), sd((n_pairs, LANES), F32), sd((n_pairs, LANES), F32)],
        scratch_shapes=[buf] * (2 * S5_STAGES) + [state, state],
        compiler_params=_params("arbitrary"),
        name="s5_prompt",
    )(u, u, bq.astype(BF16), cq.astype(BF16), lb_re, lb_im, d_skip.reshape(1, d), h0_re, h0_im)


def _s5_sample_kernel(u_ref, bq_ref, cq_ref, lbre_ref, lbim_ref, dskip_ref, h0re_ref, h0im_ref,
                      y_ref, hre_out, him_out, bre_scr, bim_scr, *, nt, nb):
    rows = nt * nb
    n_pairs = lbre_ref.shape[0]
    _s5_input_phase(u_ref, bq_ref, bre_scr, bim_scr, rows, rows, True)

    def pair_body(pr, carry):
        lr = lbre_ref[pl.ds(pr, 1), :]
        li = lbim_ref[pl.ds(pr, 1), :]
        hr = h0re_ref[pr]
        hi = h0im_ref[pr]
        for t in range(nt):
            idx = pl.ds(pl.multiple_of(pr * rows + t * nb, 8), nb)
            nr = lr * hr - li * hi + bre_scr[idx, :]
            ni = lr * hi + li * hr + bim_scr[idx, :]
            bre_scr[idx, :] = nr
            bim_scr[idx, :] = ni
            hr, hi = nr, ni
        hre_out[pr] = hr
        him_out[pr] = hi
        return carry

    lax.fori_loop(0, n_pairs, pair_body, 0)
    _s5_output_phase(u_ref, cq_ref, dskip_ref, y_ref, bre_scr, bim_scr, rows, rows, False)


def s5_sample(u, bq, cq, lb_re, lb_im, d_skip, h0_re, h0_im, nt, nb):
    rows, d = u.shape
    n_pairs = lb_re.shape[0]
    sd = jax.ShapeDtypeStruct
    return pl.pallas_call(
        functools.partial(_s5_sample_kernel, nt=nt, nb=nb),
        out_shape=[sd((rows, d), F32), sd((n_pairs, nb, LANES), F32), sd((n_pairs, nb, LANES), F32)],
        scratch_shapes=[pltpu.VMEM((n_pairs * rows, LANES), F32), pltpu.VMEM((n_pairs * rows, LANES), F32)],
        compiler_params=pltpu.CompilerParams(vmem_limit_bytes=VMEM_LIMIT),
        name="s5_sample",
    )(u, bq, cq.astype(BF16), lb_re, lb_im, d_skip.reshape(1, d), h0_re, h0_im)


def _rms(x, g):
    ms = jnp.mean(x * x, axis=-1, keepdims=True)
    return x * lax.rsqrt(ms + EPS) * g


def _mix_kernel(att_ref, y_ref, ga_ref, gs_ref, wglu_ref, bglu_ref, gatt_ref, gssm_ref,
                attn_ref, ssmn_ref):
    attn_ref[...] = (_rms(att_ref[...], gatt_ref[...]) * jax.nn.silu(ga_ref[...])).astype(attn_ref.dtype)
    y = jax.nn.gelu(y_ref[...])
    y = y * jax.nn.sigmoid(_dot(y.astype(BF16), wglu_ref[...]) + bglu_ref[...])
    ssmn_ref[...] = (_rms(y, gssm_ref[...]) * jax.nn.silu(gs_ref[...])).astype(ssmn_ref.dtype)


def mix_post(att, y, gug, w_glu, b_glu, g_att, g_ssm, tm=256):
    m, d = att.shape
    tm = min(tm, m)
    row = lambda c: pl.BlockSpec((tm, d), lambda i: (i, c))
    vec = pl.BlockSpec((1, d), lambda i: (0, 0))
    return pl.pallas_call(
        _mix_kernel,
        grid=(m // tm,),
        in_specs=[row(0), row(0), row(0), row(2),
                  pl.BlockSpec((d, d), lambda i: (0, 0)), vec, vec, vec],
        out_specs=[row(0), row(0)],
        out_shape=[jax.ShapeDtypeStruct((m, d), BF16), jax.ShapeDtypeStruct((m, d), BF16)],
        compiler_params=_params("parallel"),
        name="mix_post",
    )(att, y, gug, gug, w_glu, b_glu.reshape(1, d), g_att.reshape(1, d), g_ssm.reshape(1, d))


def _out_kernel(x_ref, a_ref, s_ref, wa_ref, ws_ref, g_ref, o_ref, hg_ref, ssq_ref):
    j = pl.program_id(1)
    h = x_ref[...] + (_dot(a_ref[...], wa_ref[...]) + _dot(s_ref[...], ws_ref[...]))
    o_ref[...] = h
    hg_ref[...] = (h * g_ref[...]).astype(hg_ref.dtype)
    part = jnp.sum(h * h, axis=-1, keepdims=True)

    @pl.when(j == 0)
    def _():
        ssq_ref[...] = part

    @pl.when(j > 0)
    def _():
        ssq_ref[...] += part


def out_project(x, att_n, ssm_n, w_out, g_next, tm=1024, tn=512):
    m, dm = x.shape
    dh = att_n.shape[1]
    tm = min(tm, m)
    tn = min(tn, dm)
    return pl.pallas_call(
        _out_kernel,
        grid=(m // tm, dm // tn),
        in_specs=[pl.BlockSpec((tm, tn), lambda i, j: (i, j)),
                  pl.BlockSpec((tm, dh), lambda i, j: (i, 0)),
                  pl.BlockSpec((tm, dh), lambda i, j: (i, 0)),
                  pl.BlockSpec((dh, tn), lambda i, j: (0, j)),
                  pl.BlockSpec((dh, tn), lambda i, j: (1, j)),
                  pl.BlockSpec((1, tn), lambda i, j: (0, j))],
        out_specs=[pl.BlockSpec((tm, tn), lambda i, j: (i, j)),
                   pl.BlockSpec((tm, tn), lambda i, j: (i, j)),
                   pl.BlockSpec((tm, 1), lambda i, j: (i, 0))],
        out_shape=[jax.ShapeDtypeStruct((m, dm), F32), jax.ShapeDtypeStruct((m, dm), BF16),
                   jax.ShapeDtypeStruct((m, 1), F32)],
        compiler_params=_params("parallel", "arbitrary"),
        name="out_proj",
    )(x, att_n, ssm_n, w_out, w_out, g_next.reshape(1, dm))


def _ple_kernel(h_ref, hg_ref, ssq_ref, p_ref, wg_ref, wp_ref, o_ref):
    inv = lax.rsqrt(ssq_ref[...] * (1.0 / hg_ref.shape[1]) + EPS)
    gate = jax.nn.sigmoid(inv * _dot(hg_ref[...], wg_ref[...]))
    ple = _dot(p_ref[...].astype(BF16), wp_ref[...])
    o_ref[...] = h_ref[...] + ple * gate


def ple_gate(h, hg, ssq, p, w_gate, w_ple, tm=1024, tn=512):
    m, dm = h.shape
    dp = p.shape[1]
    tm = min(tm, m)
    tn = min(tn, dm)
    return pl.pallas_call(
        _ple_kernel,
        grid=(m // tm, dm // tn),
        in_specs=[pl.BlockSpec((tm, tn), lambda i, j: (i, j)),
                  pl.BlockSpec((tm, dm), lambda i, j: (i, 0)),
                  pl.BlockSpec((tm, 1), lambda i, j: (i, 0)),
                  pl.BlockSpec((tm, dp), lambda i, j: (i, 0)),
                  pl.BlockSpec((dm, tn), lambda i, j: (0, j)),
                  pl.BlockSpec((dp, tn), lambda i, j: (0, j))],
        out_specs=pl.BlockSpec((tm, tn), lambda i, j: (i, j)),
        out_shape=jax.ShapeDtypeStruct((m, dm), F32),
        compiler_params=_params("parallel", "arbitrary"),
        name="ple_gate",
    )(h, hg, ssq, p, w_gate, w_ple)


def _pair_layout(x):
    return x.reshape(*x.shape[:-2], x.shape[-2] // 2, 2 * x.shape[-1])


def kernel(x_prompt, x_sample, cache_k, cache_v, state_ssm_re, state_ssm_im, page_table, p_prompt, p_sample, g_in, w_in, att_bias, a_re, a_im, log_dt, b_re, b_im, c_re, c_im, d_skip, w_glu, b_glu, g_att, g_ssm, w_out, g_ple, w_ple, w_ple_gate, g_final):
    depth = w_in.shape[0]
    assert depth == 1, "single-layer step"
    nb_p, t_p, dm = x_prompt.shape
    assert nb_p == 1, "one prompt sequence"
    nb, nt, _ = x_sample.shape
    n_groups, n_state = a_re.shape[1:]
    d_ssm = n_groups * GROUP_CH
    d_att = dm - d_ssm
    nh = d_att // HEAD_DIM
    scale = 1.0 / math.sqrt(HEAD_DIM)

    w_in_b = w_in[0].astype(BF16)
    w_glu_b = w_glu[0].astype(BF16)
    w_out_b = w_out[0].astype(BF16)
    w_ple_b = w_ple[0].astype(BF16)
    w_gate_b = w_ple_gate[0].astype(BF16)

    lb_re, lb_im, bb_re, bb_im = s5_discretize(a_re[0], a_im[0], log_dt[0], b_re[0], b_im[0])
    bq, cq = _s5_block_operands(bb_re, bb_im, c_re[0], c_im[0])
    lb_re_p, lb_im_p = _pair_layout(lb_re), _pair_layout(lb_im)
    dsk = d_skip[0].reshape(-1)

    def in_proj(x2d):
        xn = rmsnorm_rows(x2d, g_in[0], BF16)
        (q,) = project(xn, w_in_b, 0, d_att, [BF16], scale=scale)
        k, k_b = project(xn, w_in_b, d_att, d_att, [F32, BF16])
        v, v_b = project(xn, w_in_b, 2 * d_att, d_att, [F32, BF16])
        (gug,) = project(xn, w_in_b, 3 * d_att, d_att + 2 * d_ssm, [F32])
        return q, k, k_b, v, v_b, gug

    def tail(x2d, p2d, att, y, gug):
        att_n, ssm_n = mix_post(att, y, gug, w_glu_b, b_glu[0], g_att[0], g_ssm[0])
        h2, h2g, ssq2 = out_project(x2d, att_n, ssm_n, w_out_b, g_ple[0])
        h3 = ple_gate(h2, h2g, ssq2, p2d, w_gate_b, w_ple_b)
        return rmsnorm_rows(h3, g_final, F32)

    xp = x_prompt.reshape(t_p, dm)
    q, k, k_b, v, v_b, gug = in_proj(xp)
    att = attention_prompt(q, k_b, v_b, att_bias[0])
    zero_state = jnp.zeros((n_groups // 2, 2 * n_state), F32)
    y, hre_p, him_p = s5_prompt(gug, d_att, bq, cq, lb_re_p, lb_im_p, dsk, zero_state, zero_state)
    y_prompt = tail(xp, p_prompt[0, 0], att, y, gug).reshape(1, t_p, dm)
    new_k_prompt = k.reshape(1, 1, t_p, nh, HEAD_DIM)
    new_v_prompt = v.reshape(1, 1, t_p, nh, HEAD_DIM)
    new_re_prompt = hre_p.reshape(1, 1, n_groups, n_state)
    new_im_prompt = him_p.reshape(1, 1, n_groups, n_state)

    xs = x_sample.reshape(nb * nt, dm)
    q, k, _, v, _, gug = in_proj(xs)
    att = attention_sample(
        q.reshape(nb, nt, d_att), k.reshape(nb, nt, d_att), v.reshape(nb, nt, d_att),
        cache_k[0], cache_v[0], page_table, att_bias[0]).reshape(nb * nt, d_att)
    u_tb = gug[:, d_att:d_att + d_ssm].reshape(nb, nt, d_ssm).transpose(1, 0, 2).reshape(nt * nb, d_ssm)
    h0_re = _pair_layout(state_ssm_re[0]).transpose(1, 0, 2)
    h0_im = _pair_layout(state_ssm_im[0]).transpose(1, 0, 2)
    y_tb, hre_s, him_s = s5_sample(u_tb, bq, cq, lb_re_p, lb_im_p, dsk, h0_re, h0_im, nt, nb)
    y = y_tb.reshape(nt, nb, d_ssm).transpose(1, 0, 2).reshape(nb * nt, d_ssm)
    y_sample = tail(xs, p_sample[0].reshape(nb * nt, -1), att, y, gug).reshape(nb, nt, dm)
    new_k_sample = k.reshape(1, nb, nt, nh, HEAD_DIM)
    new_v_sample = v.reshape(1, nb, nt, nh, HEAD_DIM)
    new_re_sample = hre_s.transpose(1, 0, 2).reshape(1, nb, n_groups, n_state)
    new_im_sample = him_s.transpose(1, 0, 2).reshape(1, nb, n_groups, n_state)

    return (y_prompt, y_sample, new_k_prompt, new_v_prompt, new_k_sample, new_v_sample,
            new_re_prompt, new_im_prompt, new_re_sample, new_im_sample)
```

```python
import functools
import math

import jax
import jax.numpy as jnp
from jax import lax
from jax.experimental import pallas as pl
from jax.experimental.pallas import tpu as pltpu

F32 = jnp.float32
BF16 = jnp.bfloat16

HEAD_DIM = 128
GROUP_CH = 16
STATE_N = 64
EPS = 1e-6
LANES = 128
VMEM_LIMIT = 56 * 1024 * 1024
GROUPS_PER_QUAD = LANES // GROUP_CH
PAIRS_PER_QUAD = GROUPS_PER_QUAD // 2
SUBLANES = 8
HEAD_BLOCK = SUBLANES
STRIDE_PAD = 4


def _params(*sem):
    return pltpu.CompilerParams(dimension_semantics=sem, vmem_limit_bytes=VMEM_LIMIT)


def _dot(a, b):
    return jnp.dot(a, b, preferred_element_type=F32)


def _split_bf16(x):
    hi = x.astype(BF16)
    lo = (x - hi.astype(F32)).astype(BF16)
    return hi, lo


def _dot_x3(a, b):
    ah, al = _split_bf16(a)
    bh, bl = _split_bf16(b)
    return _dot(ah, bh) + _dot(al, bh) + _dot(ah, bl)


def _softplus(z):
    return jnp.maximum(z, 0.0) + jnp.log(1.0 + jnp.exp2(jnp.abs(z) * -math.log2(math.e)))


def _rms_kernel(x_ref, g_ref, o_ref):
    x = x_ref[...]
    ms = jnp.mean(x * x, axis=-1, keepdims=True)
    o_ref[...] = (x * lax.rsqrt(ms + EPS) * g_ref[...]).astype(o_ref.dtype)


def rmsnorm_rows(x, g, out_dtype, tm=256):
    m, d = x.shape
    tm = min(tm, m)
    return pl.pallas_call(
        _rms_kernel,
        grid=(m // tm,),
        in_specs=[pl.BlockSpec((tm, d), lambda i: (i, 0)),
                  pl.BlockSpec((1, d), lambda i: (0, 0))],
        out_specs=pl.BlockSpec((tm, d), lambda i: (i, 0)),
        out_shape=jax.ShapeDtypeStruct((m, d), out_dtype),
        compiler_params=_params("parallel"),
        name="rmsnorm",
    )(x, g.reshape(1, d))


def _proj_kernel(x_ref, w_ref, *o_refs, scale):
    acc = _dot(x_ref[...], w_ref[...])
    if scale != 1.0:
        acc = acc * scale
    for o_ref in o_refs:
        o_ref[...] = acc.astype(o_ref.dtype)


def project(xn, w, col0, n, out_dtypes, scale=1.0, tm=1024, tn=1024):
    m, k = xn.shape
    tm = min(tm, m)
    tn = math.gcd(tn, n, col0)
    off = col0 // tn
    return pl.pallas_call(
        functools.partial(_proj_kernel, scale=scale),
        grid=(m // tm, n // tn),
        in_specs=[pl.BlockSpec((tm, k), lambda i, j: (i, 0)),
                  pl.BlockSpec((k, tn), lambda i, j: (0, off + j))],
        out_specs=[pl.BlockSpec((tm, tn), lambda i, j: (i, j)) for _ in out_dtypes],
        out_shape=[jax.ShapeDtypeStruct((m, n), dt) for dt in out_dtypes],
        compiler_params=_params("parallel", "arbitrary"),
        name="in_proj",
    )(xn, w)


def _tail_matrix(tk, row_sums, split):
    j = jnp.arange(tk)[:, None]
    s = jnp.arange(tk)[None, :]
    u = (j > s).astype(BF16)
    if row_sums:
        u = jnp.concatenate([u, jnp.ones((tk, LANES), BF16)], axis=1)
    return jnp.concatenate([u, u], axis=0) if split else u


def _suffix_sums(sp, uo):
    if uo.shape[0] == 2 * sp.shape[1]:
        return _dot(jnp.concatenate(_split_bf16(sp), axis=1), uo)
    return _dot(sp.astype(BF16), uo)


def _sb_tile(q, k, v, bias, uo, r, mask, tk):
    s = lax.dot_general(q, k, (((1,), (1,)), ((), ())), preferred_element_type=F32)
    z = s + bias
    sp = _softplus(z)
    if mask is not None:
        sp = jnp.where(mask, sp, 0.0)
    t2 = _suffix_sums(sp, uo)
    rb = jnp.concatenate([r] * (tk // LANES), axis=1) if tk > LANES else r
    w = jnp.exp(z - sp - t2[:, :tk] - rb)
    if mask is not None:
        w = jnp.where(mask, w, 0.0)
    if uo.shape[1] > tk:
        rs = t2[:, tk:]
    else:
        rs = jnp.broadcast_to(jnp.sum(sp, axis=1, keepdims=True), r.shape)
    return _dot(w.astype(BF16), v), rs


def _attn_kernel(bias_ref, q_ref, k_ref, v_ref, uo_ref, o_ref, acc_ref, r_ref, *, tq, tk):
    h = pl.program_id(0)
    qi = pl.program_id(1)
    bias = bias_ref[h]
    q = q_ref[...]
    uo = uo_ref[...]
    acc_ref[...] = jnp.zeros_like(acc_ref)
    r_ref[...] = jnp.zeros_like(r_ref)
    nd = tq // tk

    def tile(kt, row0, mask):
        start = pl.multiple_of(kt * tk, tk)
        k = k_ref[pl.ds(start, tk), :]
        v = v_ref[pl.ds(start, tk), :]
        r = r_ref[row0:, :]
        pv, rs = _sb_tile(q[row0:, :], k, v, bias, uo, r, mask, tk)
        acc_ref[row0:, :] += pv
        r_ref[row0:, :] = r + rs

    for d in range(nd - 1, -1, -1):
        row0 = d * tk
        row = lax.broadcasted_iota(jnp.int32, (tq - row0, tk), 0)
        col = lax.broadcasted_iota(jnp.int32, (tq - row0, tk), 1)
        tile(qi * nd + d, row0, col < row)

    def body(i, carry):
        for d in range(nd):
            tile((qi - i) * nd - 1 - d, 0, None)
        return carry

    lax.fori_loop(0, qi, body, 0)
    o_ref[...] = acc_ref[...]


def attention_prompt(q, k, v, bias, tq=2048, tk=256):
    t, d = q.shape
    nh = d // HEAD_DIM
    tq = min(tq, t)
    tk = min(tk, tq)
    return pl.pallas_call(
        functools.partial(_attn_kernel, tq=tq, tk=tk),
        grid=(nh, t // tq),
        in_specs=[pl.BlockSpec(memory_space=pltpu.SMEM),
                  pl.BlockSpec((tq, HEAD_DIM), lambda h, i: (i, h)),
                  pl.BlockSpec((t, HEAD_DIM), lambda h, i: (0, h)),
                  pl.BlockSpec((t, HEAD_DIM), lambda h, i: (0, h)),
                  pl.BlockSpec((tk, tk), lambda h, i: (0, 0))],
        out_specs=pl.BlockSpec((tq, HEAD_DIM), lambda h, i: (i, h)),
        out_shape=jax.ShapeDtypeStruct((t, d), F32),
        scratch_shapes=[pltpu.VMEM((tq, HEAD_DIM), F32), pltpu.VMEM((tq, LANES), F32)],
        compiler_params=_params("parallel", "arbitrary"),
        name="sb_attention_prompt",
    )(bias, q, k, v, _tail_matrix(tk, row_sums=False, split=False))


def _dec_attn_kernel(pt_ref, q_ref, bias_ref, knew_ref, vnew_ref, *rest, nt, nh, page, gp):
    del pt_ref
    nblk = nh // HEAD_BLOCK
    npb = gp * nblk
    flat = lambda ref: ref.reshape(page * HEAD_BLOCK, HEAD_DIM)
    k_pages = [[flat(r) for r in rest[g * nblk:(g + 1) * nblk]] for g in range(gp)]
    v_pages = [[flat(r) for r in rest[npb + g * nblk:npb + (g + 1) * nblk]] for g in range(gp)]
    uo_ref, o_ref, acc_ref, r_ref, kpad_ref, vpad_ref = rest[2 * npb:]
    j = pl.program_id(1)
    rows = nh * nt
    hpg = SUBLANES // nt
    ng = rows // SUBLANES
    bias = bias_ref[...]
    uo = uo_ref[...]
    sub_head = lax.broadcasted_iota(jnp.int32, (SUBLANES, page), 0) // nt

    def head_rows(blocks, g):
        parts = []
        for p in range(hpg):
            blk, hb = divmod(g * hpg + p, HEAD_BLOCK)
            parts.append(blocks[blk][pl.ds(hb, page, stride=HEAD_BLOCK), :])
        return jnp.concatenate(parts, axis=0).astype(BF16)

    def process(k_ref, v_ref, mask):
        zs = []
        for g in range(ng):
            qg = q_ref[g * SUBLANES:(g + 1) * SUBLANES, :].astype(BF16)
            s = lax.dot_general(qg, head_rows(k_ref, g), (((1,), (1,)), ((), ())),
                                preferred_element_type=F32)
            z = s[:, :page]
            for p in range(1, hpg):
                z = jnp.where(sub_head == p, s[:, p * page:(p + 1) * page], z)
            zs.append(z)
        z = jnp.concatenate(zs, axis=0) + bias
        sp = _softplus(z)
        if mask is not None:
            sp = jnp.where(mask, sp, 0.0)
        t2 = _suffix_sums(sp, uo)
        r = r_ref[...]
        w = jnp.exp(z - sp - t2[:, :page] - r)
        if mask is not None:
            w = jnp.where(mask, w, 0.0)
        for g in range(ng):
            wg = w[g * SUBLANES:(g + 1) * SUBLANES, :]
            wexp = jnp.concatenate([jnp.where(sub_head == p, wg, 0.0) for p in range(hpg)],
                                   axis=1).astype(BF16)
            acc_ref[g * SUBLANES:(g + 1) * SUBLANES, :] += _dot(wexp, head_rows(v_ref, g))
        r_ref[...] = r + t2[:, page:]

    @pl.when(j == 0)
    def _():
        acc_ref[...] = jnp.zeros_like(acc_ref)
        r_ref[...] = jnp.zeros_like(r_ref)
        kpad_ref[...] = jnp.zeros_like(kpad_ref)
        vpad_ref[...] = jnp.zeros_like(vpad_ref)
        kpad_ref[:, 0:nt * HEAD_BLOCK, :] = knew_ref[...]
        vpad_ref[:, 0:nt * HEAD_BLOCK, :] = vnew_ref[...]
        tok = lax.broadcasted_iota(jnp.int32, (rows, page), 0) % nt
        key = lax.broadcasted_iota(jnp.int32, (rows, page), 1)
        process([kpad_ref.at[bk] for bk in range(nblk)],
                [vpad_ref.at[bk] for bk in range(nblk)], key < tok)

    @pl.when(j > 0)
    def _():
        for g in range(gp):
            process(k_pages[g], v_pages[g], None)

    @pl.when(j == pl.num_programs(1) - 1)
    def _():
        o_ref[...] = acc_ref[...]


def attention_sample(q, k_new, v_new, cache_k, cache_v, page_table, bias, pages_per_step=8):
    b, nt, d = q.shape
    nh = d // HEAD_DIM
    n_pool, page = cache_k.shape[:2]
    n_pages = page_table.shape[1]
    rows = nh * nt
    gp = math.gcd(n_pages, pages_per_step)
    assert SUBLANES % nt == 0 and nh % HEAD_BLOCK == 0
    nblk = nh // HEAD_BLOCK
    q_rows = q.astype(F32).reshape(b, nt, nh, HEAD_DIM).transpose(0, 2, 1, 3).reshape(b, rows, HEAD_DIM)
    bias_rows = jnp.broadcast_to(jnp.repeat(bias, nt)[:, None], (rows, LANES)).astype(F32)
    cache_k = cache_k.reshape(n_pool, page, nh, HEAD_DIM)
    cache_v = cache_v.reshape(n_pool, page, nh, HEAD_DIM)
    blocked = lambda x: x.reshape(b, nt, nblk, HEAD_BLOCK, HEAD_DIM).transpose(0, 2, 1, 3, 4).reshape(
        b, nblk, nt * HEAD_BLOCK, HEAD_DIM)

    def page_spec(g, bk):
        def index_map(i, j, pt):
            return (pt[i, n_pages - 1 - (jnp.maximum(j, 1) - 1) * gp - g], 0, bk, 0)
        return pl.BlockSpec((None, page, HEAD_BLOCK, HEAD_DIM), index_map)

    page_specs = [page_spec(g, bk) for g in range(gp) for bk in range(nblk)]
    per_seq = lambda *shape: pl.BlockSpec((None, *shape), lambda i, j, pt: (i,) + (0,) * len(shape))
    pad_scratch = pltpu.VMEM((nblk, page * HEAD_BLOCK, HEAD_DIM), F32)
    grid_spec = pltpu.PrefetchScalarGridSpec(
        num_scalar_prefetch=1,
        grid=(b, n_pages // gp + 1),
        in_specs=[per_seq(rows, HEAD_DIM),
                  pl.BlockSpec((rows, LANES), lambda i, j, pt: (0, 0)),
                  per_seq(nblk, nt * HEAD_BLOCK, HEAD_DIM), per_seq(nblk, nt * HEAD_BLOCK, HEAD_DIM),
                  *page_specs, *page_specs,
                  pl.BlockSpec((2 * page, page + LANES), lambda i, j, pt: (0, 0))],
        out_specs=per_seq(rows, HEAD_DIM),
        scratch_shapes=[pltpu.VMEM((rows, HEAD_DIM), F32), pltpu.VMEM((rows, LANES), F32),
                        pad_scratch, pad_scratch],
    )
    out = pl.pallas_call(
        functools.partial(_dec_attn_kernel, nt=nt, nh=nh, page=page, gp=gp),
        grid_spec=grid_spec,
        out_shape=jax.ShapeDtypeStruct((b, rows, HEAD_DIM), F32),
        compiler_params=_params("parallel", "arbitrary"),
        name="sb_attention_sample",
    )(page_table, q_rows, bias_rows, blocked(k_new), blocked(v_new),
      *([cache_k] * (gp * nblk)), *([cache_v] * (gp * nblk)), _tail_matrix(page, row_sums=True, split=True))
    return out.reshape(b, nh, nt, HEAD_DIM).transpose(0, 2, 1, 3).reshape(b, nt, d)


def _s5_discretize_kernel(are_ref, aim_ref, ldt_ref, bre_ref, bim_ref,
                          lbre_ref, lbim_ref, bbre_ref, bbim_ref):
    a_re = are_ref[...]
    a_im = aim_ref[...]
    dt = jnp.exp(ldt_ref[...])
    mag = jnp.exp(a_re * dt)
    ang = a_im * dt
    lb_re = mag * jnp.cos(ang)
    lb_im = mag * jnp.sin(ang)
    den = a_re * a_re + a_im * a_im
    num_re = lb_re - 1.0
    f_re = (num_re * a_re + lb_im * a_im) / den
    f_im = (lb_im * a_re - num_re * a_im) / den
    lbre_ref[...] = lb_re
    lbim_ref[...] = lb_im
    b_re = bre_ref[...]
    b_im = bim_ref[...]
    bbre_ref[...] = f_re * b_re - f_im * b_im
    bbim_ref[...] = f_re * b_im + f_im * b_re


def s5_discretize(a_re, a_im, log_dt, b_re, b_im):
    g, n = a_re.shape
    p = b_re.shape[-1]
    rep = lambda x: jnp.repeat(x, p, axis=0)
    rows = lambda x: x.transpose(0, 2, 1).reshape(g * p, n)
    sd = jax.ShapeDtypeStruct((g * p, n), F32)
    lb_re, lb_im, bb_re, bb_im = pl.pallas_call(
        _s5_discretize_kernel,
        out_shape=[sd, sd, sd, sd],
        name="s5_discretize",
    )(rep(a_re), rep(a_im), rep(jnp.broadcast_to(log_dt[:, None], (g, n))), rows(b_re), rows(b_im))
    return lb_re[::p], lb_im[::p], bb_re.reshape(g, p, n), bb_im.reshape(g, p, n)


def _s5_block_operands(bb_re, bb_im, c_re, c_im):
    g, p, n = bb_re.shape
    nq = g // GROUPS_PER_QUAD
    eye = jnp.eye(GROUPS_PER_QUAD, dtype=F32)

    def b_side(x):
        x = x.reshape(nq, GROUPS_PER_QUAD, p, n)
        return (x[:, :, :, None, :] * eye[None, :, None, :, None]).reshape(
            nq, GROUPS_PER_QUAD * p, GROUPS_PER_QUAD * n)

    def c_side(x):
        x = x.reshape(nq, GROUPS_PER_QUAD, p, n).transpose(0, 1, 3, 2)
        return (x[:, :, :, None, :] * eye[None, :, None, :, None]).reshape(
            nq, GROUPS_PER_QUAD * n, GROUPS_PER_QUAD * p)

    bq = jnp.concatenate([b_side(bb_re), b_side(bb_im)], axis=2)
    cq = jnp.concatenate([c_side(c_re), -c_side(c_im)], axis=1)
    return bq, cq


def _s5_input_phase(u_ref, bq_ref, bre_scr, bim_scr, rows, pitch, precise):
    nq = bq_ref.shape[0]
    half = PAIRS_PER_QUAD * LANES
    for qd in range(nq):
        _s5_input_quad(u_ref, bq_ref, bre_scr, bim_scr, qd, rows, pitch, precise)


def _s5_input_quad(u_ref, bq_ref, bre_scr, bim_scr, qd, rows, pitch, precise):
    half = PAIRS_PER_QUAD * LANES
    uq = u_ref[:, qd * LANES:(qd + 1) * LANES]
    if precise:
        bu = _dot_x3(uq, bq_ref[qd])
    else:
        bu = _dot(uq.astype(BF16), bq_ref[qd])
    for jj in range(PAIRS_PER_QUAD):
        r0 = (PAIRS_PER_QUAD * qd + jj) * pitch
        bre_scr[r0:r0 + rows, :] = bu[:, jj * LANES:(jj + 1) * LANES]
        bim_scr[r0:r0 + rows, :] = bu[:, half + jj * LANES:half + (jj + 1) * LANES]


def _s5_output_phase(u_ref, cq_ref, dskip_ref, y_ref, hre_scr, him_scr, rows, pitch, precise):
    for qd in range(cq_ref.shape[0]):
        _s5_output_quad(u_ref, cq_ref, dskip_ref, y_ref, hre_scr, him_scr, qd, rows, pitch, precise)


def _s5_output_quad(u_ref, cq_ref, dskip_ref, y_ref, hre_scr, him_scr, qd, rows, pitch, precise):
    parts = []
    for scr in (hre_scr, him_scr):
        for jj in range(PAIRS_PER_QUAD):
            r0 = (PAIRS_PER_QUAD * qd + jj) * pitch
            parts.append(scr[r0:r0 + rows, :])
    hcat = jnp.concatenate(parts, axis=1)
    if precise:
        yq = _dot_x3(hcat, cq_ref[qd])
    else:
        yq = _dot(hcat.astype(BF16), cq_ref[qd])
    sl = slice(qd * LANES, (qd + 1) * LANES)
    y_ref[:, sl] = yq + dskip_ref[:, sl] * u_ref[:, sl]


def _s5_prompt_kernel(u_ref, bq_ref, cq_ref, lbre_ref, lbim_ref, dskip_ref, h0re_ref, h0im_ref,
                      y_ref, hre_out, him_out, bre_scr, bim_scr, stre_scr, stim_scr, *, tc):
    i = pl.program_id(0)
    n_pairs = lbre_ref.shape[0]

    @pl.when(i == 0)
    def _():
        stre_scr[...] = h0re_ref[...]
        stim_scr[...] = h0im_ref[...]

    pitch = tc + STRIDE_PAD
    _s5_input_phase(u_ref, bq_ref, bre_scr, bim_scr, tc, pitch, False)

    lr = lbre_ref[...]
    li = lbim_ref[...]

    def step(t, carry):
        hr, hi = carry
        idx = pl.ds(t, n_pairs, stride=pitch)
        nr = lr * hr - li * hi + bre_scr[idx, :]
        ni = lr * hi + li * hr + bim_scr[idx, :]
        bre_scr[idx, :] = nr
        bim_scr[idx, :] = ni
        return nr, ni

    hr, hi = lax.fori_loop(0, tc, step, (stre_scr[...], stim_scr[...]))
    stre_scr[...] = hr
    stim_scr[...] = hi
    hre_out[...] = hr
    him_out[...] = hi

    _s5_output_phase(u_ref, cq_ref, dskip_ref, y_ref, bre_scr, bim_scr, tc, pitch, False)


def s5_prompt(u, col0, bq, cq, lb_re, lb_im, d_skip, h0_re, h0_im, tc=256):
    t = u.shape[0]
    n_pairs = lb_re.shape[0]
    d = n_pairs * 2 * GROUP_CH
    tc = min(tc, t)
    whole = lambda *shape: pl.BlockSpec(shape, lambda i: (0,) * len(shape))
    sd = jax.ShapeDtypeStruct
    return pl.pallas_call(
        functools.partial(_s5_prompt_kernel, tc=tc),
        grid=(t // tc,),
        in_specs=[pl.BlockSpec((tc, d), lambda i: (i, col0 // d)),
                  whole(*bq.shape), whole(*cq.shape),
                  whole(n_pairs, LANES), whole(n_pairs, LANES), whole(1, d),
                  whole(n_pairs, LANES), whole(n_pairs, LANES)],
        out_specs=[pl.BlockSpec((tc, d), lambda i: (i, 0)),
                   whole(n_pairs, LANES), whole(n_pairs, LANES)],
        out_shape=[sd((t, d), F32), sd((n_pairs, LANES), F32), sd((n_pairs, LANES), F32)],
        scratch_shapes=[pltpu.VMEM((n_pairs * (tc + STRIDE_PAD), LANES), F32),
                        pltpu.VMEM((n_pairs * (tc + STRIDE_PAD), LANES), F32),
                        pltpu.VMEM((n_pairs, LANES), F32), pltpu.VMEM((n_pairs, LANES), F32)],
        compiler_params=_params("arbitrary"),
        name="s5_prompt",
    )(u, bq.astype(BF16), cq.astype(BF16), lb_re, lb_im, d_skip.reshape(1, d), h0_re, h0_im)


def _s5_sample_kernel(u_ref, bq_ref, cq_ref, lbre_ref, lbim_ref, dskip_ref, h0re_ref, h0im_ref,
                      y_ref, hre_out, him_out, bre_scr, bim_scr, *, nt, nb):
    rows = nt * nb
    n_pairs = lbre_ref.shape[0]
    _s5_input_phase(u_ref, bq_ref, bre_scr, bim_scr, rows, rows, True)

    def pair_body(pr, carry):
        lr = lbre_ref[pl.ds(pr, 1), :]
        li = lbim_ref[pl.ds(pr, 1), :]
        hr = h0re_ref[pr]
        hi = h0im_ref[pr]
        for t in range(nt):
            idx = pl.ds(pl.multiple_of(pr * rows + t * nb, 8), nb)
            nr = lr * hr - li * hi + bre_scr[idx, :]
            ni = lr * hi + li * hr + bim_scr[idx, :]
            bre_scr[idx, :] = nr
            bim_scr[idx, :] = ni
            hr, hi = nr, ni
        hre_out[pr] = hr
        him_out[pr] = hi
        return carry

    lax.fori_loop(0, n_pairs, pair_body, 0)
    _s5_output_phase(u_ref, cq_ref, dskip_ref, y_ref, bre_scr, bim_scr, rows, rows, False)


def s5_sample(u, bq, cq, lb_re, lb_im, d_skip, h0_re, h0_im, nt, nb):
    rows, d = u.shape
    n_pairs = lb_re.shape[0]
    sd = jax.ShapeDtypeStruct
    return pl.pallas_call(
        functools.partial(_s5_sample_kernel, nt=nt, nb=nb),
        out_shape=[sd((rows, d), F32), sd((n_pairs, nb, LANES), F32), sd((n_pairs, nb, LANES), F32)],
        scratch_shapes=[pltpu.VMEM((n_pairs * rows, LANES), F32), pltpu.VMEM((n_pairs * rows, LANES), F32)],
        compiler_params=pltpu.CompilerParams(vmem_limit_bytes=VMEM_LIMIT),
        name="s5_sample",
    )(u, bq, cq.astype(BF16), lb_re, lb_im, d_skip.reshape(1, d), h0_re, h0_im)


def _rms(x, g):
    ms = jnp.mean(x * x, axis=-1, keepdims=True)
    return x * lax.rsqrt(ms + EPS) * g


def _mix_kernel(att_ref, y_ref, ga_ref, gs_ref, wglu_ref, bglu_ref, gatt_ref, gssm_ref,
                attn_ref, ssmn_ref):
    attn_ref[...] = (_rms(att_ref[...], gatt_ref[...]) * jax.nn.silu(ga_ref[...])).astype(attn_ref.dtype)
    y = jax.nn.gelu(y_ref[...])
    y = y * jax.nn.sigmoid(_dot(y.astype(BF16), wglu_ref[...]) + bglu_ref[...])
    ssmn_ref[...] = (_rms(y, gssm_ref[...]) * jax.nn.silu(gs_ref[...])).astype(ssmn_ref.dtype)


def mix_post(att, y, gug, w_glu, b_glu, g_att, g_ssm, tm=256):
    m, d = att.shape
    tm = min(tm, m)
    row = lambda c: pl.BlockSpec((tm, d), lambda i: (i, c))
    vec = pl.BlockSpec((1, d), lambda i: (0, 0))
    return pl.pallas_call(
        _mix_kernel,
        grid=(m // tm,),
        in_specs=[row(0), row(0), row(0), row(2),
                  pl.BlockSpec((d, d), lambda i: (0, 0)), vec, vec, vec],
        out_specs=[row(0), row(0)],
        out_shape=[jax.ShapeDtypeStruct((m, d), BF16), jax.ShapeDtypeStruct((m, d), BF16)],
        compiler_params=_params("parallel"),
        name="mix_post",
    )(att, y, gug, gug, w_glu, b_glu.reshape(1, d), g_att.reshape(1, d), g_ssm.reshape(1, d))


def _out_kernel(x_ref, a_ref, s_ref, wa_ref, ws_ref, g_ref, o_ref, hg_ref, ssq_ref):
    j = pl.program_id(1)
    h = x_ref[...] + (_dot(a_ref[...], wa_ref[...]) + _dot(s_ref[...], ws_ref[...]))
    o_ref[...] = h
    hg_ref[...] = (h * g_ref[...]).astype(hg_ref.dtype)
    part = jnp.sum(h * h, axis=-1, keepdims=True)

    @pl.when(j == 0)
    def _():
        ssq_ref[...] = part

    @pl.when(j > 0)
    def _():
        ssq_ref[...] += part


def out_project(x, att_n, ssm_n, w_out, g_next, tm=1024, tn=512):
    m, dm = x.shape
    dh = att_n.shape[1]
    tm = min(tm, m)
    tn = min(tn, dm)
    return pl.pallas_call(
        _out_kernel,
        grid=(m // tm, dm // tn),
        in_specs=[pl.BlockSpec((tm, tn), lambda i, j: (i, j)),
                  pl.BlockSpec((tm, dh), lambda i, j: (i, 0)),
                  pl.BlockSpec((tm, dh), lambda i, j: (i, 0)),
                  pl.BlockSpec((dh, tn), lambda i, j: (0, j)),
                  pl.BlockSpec((dh, tn), lambda i, j: (1, j)),
                  pl.BlockSpec((1, tn), lambda i, j: (0, j))],
        out_specs=[pl.BlockSpec((tm, tn), lambda i, j: (i, j)),
                   pl.BlockSpec((tm, tn), lambda i, j: (i, j)),
                   pl.BlockSpec((tm, 1), lambda i, j: (i, 0))],
        out_shape=[jax.ShapeDtypeStruct((m, dm), F32), jax.ShapeDtypeStruct((m, dm), BF16),
                   jax.ShapeDtypeStruct((m, 1), F32)],
        compiler_params=_params("parallel", "arbitrary"),
        name="out_proj",
    )(x, att_n, ssm_n, w_out, w_out, g_next.reshape(1, dm))


def _ple_kernel(h_ref, hg_ref, ssq_ref, p_ref, wg_ref, wp_ref, o_ref):
    inv = lax.rsqrt(ssq_ref[...] * (1.0 / hg_ref.shape[1]) + EPS)
    gate = jax.nn.sigmoid(inv * _dot(hg_ref[...], wg_ref[...]))
    ple = _dot(p_ref[...].astype(BF16), wp_ref[...])
    o_ref[...] = h_ref[...] + ple * gate


def ple_gate(h, hg, ssq, p, w_gate, w_ple, tm=1024, tn=512):
    m, dm = h.shape
    dp = p.shape[1]
    tm = min(tm, m)
    tn = min(tn, dm)
    return pl.pallas_call(
        _ple_kernel,
        grid=(m // tm, dm // tn),
        in_specs=[pl.BlockSpec((tm, tn), lambda i, j: (i, j)),
                  pl.BlockSpec((tm, dm), lambda i, j: (i, 0)),
                  pl.BlockSpec((tm, 1), lambda i, j: (i, 0)),
                  pl.BlockSpec((tm, dp), lambda i, j: (i, 0)),
                  pl.BlockSpec((dm, tn), lambda i, j: (0, j)),
                  pl.BlockSpec((dp, tn), lambda i, j: (0, j))],
        out_specs=pl.BlockSpec((tm, tn), lambda i, j: (i, j)),
        out_shape=jax.ShapeDtypeStruct((m, dm), F32),
        compiler_params=_params("parallel", "arbitrary"),
        name="ple_gate",
    )(h, hg, ssq, p, w_gate, w_ple)


def _pair_layout(x):
    return x.reshape(*x.shape[:-2], x.shape[-2] // 2, 2 * x.shape[-1])


def kernel(x_prompt, x_sample, cache_k, cache_v, state_ssm_re, state_ssm_im, page_table, p_prompt, p_sample, g_in, w_in, att_bias, a_re, a_im, log_dt, b_re, b_im, c_re, c_im, d_skip, w_glu, b_glu, g_att, g_ssm, w_out, g_ple, w_ple, w_ple_gate, g_final):
    depth = w_in.shape[0]
    assert depth == 1, "single-layer step"
    nb_p, t_p, dm = x_prompt.shape
    assert nb_p == 1, "one prompt sequence"
    nb, nt, _ = x_sample.shape
    n_groups, n_state = a_re.shape[1:]
    d_ssm = n_groups * GROUP_CH
    d_att = dm - d_ssm
    nh = d_att // HEAD_DIM
    scale = 1.0 / math.sqrt(HEAD_DIM)

    w_in_b = w_in[0].astype(BF16)
    w_glu_b = w_glu[0].astype(BF16)
    w_out_b = w_out[0].astype(BF16)
    w_ple_b = w_ple[0].astype(BF16)
    w_gate_b = w_ple_gate[0].astype(BF16)

    lb_re, lb_im, bb_re, bb_im = s5_discretize(a_re[0], a_im[0], log_dt[0], b_re[0], b_im[0])
    bq, cq = _s5_block_operands(bb_re, bb_im, c_re[0], c_im[0])
    lb_re_p, lb_im_p = _pair_layout(lb_re), _pair_layout(lb_im)
    dsk = d_skip[0].reshape(-1)

    def in_proj(x2d):
        xn = rmsnorm_rows(x2d, g_in[0], BF16)
        (q,) = project(xn, w_in_b, 0, d_att, [BF16], scale=scale)
        k, k_b = project(xn, w_in_b, d_att, d_att, [F32, BF16])
        v, v_b = project(xn, w_in_b, 2 * d_att, d_att, [F32, BF16])
        (gug,) = project(xn, w_in_b, 3 * d_att, d_att + 2 * d_ssm, [F32])
        return q, k, k_b, v, v_b, gug

    def tail(x2d, p2d, att, y, gug):
        att_n, ssm_n = mix_post(att, y, gug, w_glu_b, b_glu[0], g_att[0], g_ssm[0])
        h2, h2g, ssq2 = out_project(x2d, att_n, ssm_n, w_out_b, g_ple[0])
        h3 = ple_gate(h2, h2g, ssq2, p2d, w_gate_b, w_ple_b)
        return rmsnorm_rows(h3, g_final, F32)

    xp = x_prompt.reshape(t_p, dm)
    q, k, k_b, v, v_b, gug = in_proj(xp)
    att = attention_prompt(q, k_b, v_b, att_bias[0])
    zero_state = jnp.zeros((n_groups // 2, 2 * n_state), F32)
    y, hre_p, him_p = s5_prompt(gug, d_att, bq, cq, lb_re_p, lb_im_p, dsk, zero_state, zero_state)
    y_prompt = tail(xp, p_prompt[0, 0], att, y, gug).reshape(1, t_p, dm)
    new_k_prompt = k.reshape(1, 1, t_p, nh, HEAD_DIM)
    new_v_prompt = v.reshape(1, 1, t_p, nh, HEAD_DIM)
    new_re_prompt = hre_p.reshape(1, 1, n_groups, n_state)
    new_im_prompt = him_p.reshape(1, 1, n_groups, n_state)

    xs = x_sample.reshape(nb * nt, dm)
    q, k, _, v, _, gug = in_proj(xs)
    att = attention_sample(
        q.reshape(nb, nt, d_att), k.reshape(nb, nt, d_att), v.reshape(nb, nt, d_att),
        cache_k[0], cache_v[0], page_table, att_bias[0]).reshape(nb * nt, d_att)
    u_tb = gug[:, d_att:d_att + d_ssm].reshape(nb, nt, d_ssm).transpose(1, 0, 2).reshape(nt * nb, d_ssm)
    h0_re = _pair_layout(state_ssm_re[0]).transpose(1, 0, 2)
    h0_im = _pair_layout(state_ssm_im[0]).transpose(1, 0, 2)
    y_tb, hre_s, him_s = s5_sample(u_tb, bq, cq, lb_re_p, lb_im_p, dsk, h0_re, h0_im, nt, nb)
    y = y_tb.reshape(nt, nb, d_ssm).transpose(1, 0, 2).reshape(nb * nt, d_ssm)
    y_sample = tail(xs, p_sample[0].reshape(nb * nt, -1), att, y, gug).reshape(nb, nt, dm)
    new_k_sample = k.reshape(1, nb, nt, nh, HEAD_DIM)
    new_v_sample = v.reshape(1, nb, nt, nh, HEAD_DIM)
    new_re_sample = hre_s.transpose(1, 0, 2).reshape(1, nb, n_groups, n_state)
    new_im_sample = him_s.transpose(1, 0, 2).reshape(1, nb, n_groups, n_state)

    return (y_prompt, y_sample, new_k_prompt, new_v_prompt, new_k_sample, new_v_sample,
            new_re_prompt, new_im_prompt, new_re_sample, new_im_sample)
```
